```python
import jax, jax.numpy as jnp
from jax import lax
import numpy as np

D_MODEL = 2048
BATCH = 1
SEQ = 16384
DEPTH = 4

GRID_W = 64
CTX_LEN = 256
ROPE_THETA = 10000.0
Q_BLOCK = 128
NORM_EPS = 1e-6
LN_EPS = 1e-5
HEAD_DIM = 128

MLA_HEADS = 6
MLA_Q_RANK = 512
MLA_KV_RANK = 256
MLA_NOPE = 128
MLA_ROPE = 64
MLA_V = 128
GQA_HEADS = 6
GQA_KV_HEADS = 2
GLA_HEADS = 4
GLA_DK = 64
GLA_DV = 128
GLA_GATE_RANK = 16
GLA_GATE_NORM = 16.0
GLA_CHUNK = 64

MIX_WIDTH = MLA_HEADS * MLA_V + GQA_HEADS * HEAD_DIM + GLA_HEADS * GLA_DV
MLA_COLS = MLA_Q_RANK + MLA_KV_RANK + MLA_ROPE
GQA_COLS = (GQA_HEADS + 2 * GQA_KV_HEADS) * HEAD_DIM
GLA_COLS = 2 * GLA_HEADS * GLA_DK + 2 * GLA_HEADS * GLA_DV + 2 * GLA_GATE_RANK
IN_COLS = MLA_COLS + GQA_COLS + GLA_COLS

N_GROUPS = 4
EXPERTS_PER_GROUP = 8
N_EXPERTS = N_GROUPS * EXPERTS_PER_GROUP
EXPERT_HIDDEN = 512
TOP_K = 2

DEEPNORM_ALPHA = (2.0 * DEPTH) ** 0.25
DEEPNORM_BETA = (8.0 * DEPTH) ** -0.25
N_MOD = 6

kernel_name = "hybrid_mla_gqa_gla_hmoe_dit_block"


def rms_norm(x, g):
    xf = x.astype(jnp.float32)
    y = xf * lax.rsqrt(jnp.mean(xf * xf, axis=-1, keepdims=True) + NORM_EPS)
    return (y * g.astype(jnp.float32)).astype(x.dtype)


def layer_norm(x, g, b):
    xf = x.astype(jnp.float32)
    mu = jnp.mean(xf, axis=-1, keepdims=True)
    var = jnp.mean(jnp.square(xf - mu), axis=-1, keepdims=True)
    y = (xf - mu) * lax.rsqrt(var + LN_EPS) * g.astype(jnp.float32) + b.astype(jnp.float32)
    return y.astype(x.dtype)


def axial_rope_tables(n_tokens, dim):
    rows = n_tokens // GRID_W
    row = jnp.repeat(jnp.arange(rows), GRID_W).astype(jnp.float32)
    col = jnp.tile(jnp.arange(GRID_W), rows).astype(jnp.float32)
    half = dim // 2
    inv_freq = ROPE_THETA ** (-jnp.arange(0, half, 2, dtype=jnp.float32) / half)
    ang_r = row[:, None] * inv_freq
    ang_c = col[:, None] * inv_freq
    ang = jnp.concatenate([ang_r, ang_r, ang_c, ang_c], axis=-1)
    return jnp.cos(ang), jnp.sin(ang)


def apply_rope(x, cos, sin):
    x1, x2, x3, x4 = jnp.split(x, 4, axis=-1)
    rot = jnp.concatenate([-x2, x1, -x4, x3], axis=-1)
    return (x.astype(jnp.float32) * cos + rot.astype(jnp.float32) * sin).astype(x.dtype)


def blocked_attention(q, k, v, scale):
    B, Hk, G, T, d = q.shape
    nb = T // Q_BLOCK
    qb = jnp.moveaxis(q.reshape(B, Hk, G, nb, Q_BLOCK, d), 3, 0)

    def attend(qi):
        s = jnp.einsum('bhgqd,bhsd->bhgqs', qi, k, preferred_element_type=jnp.float32) * scale
        p = jax.nn.softmax(s, axis=-1)
        return jnp.einsum('bhgqs,bhsd->bhgqd', p.astype(v.dtype), v)

    o = lax.map(attend, qb)
    return jnp.moveaxis(o, 0, 3).reshape(B, Hk, G, T, v.shape[-1])


def merge_heads(o):
    B, Hk, G, T, dv = o.shape
    return o.transpose(0, 3, 1, 2, 4).reshape(B, T, Hk * G * dv)


def mla_mixer(p_c, p_l, q_norm, w_uq, kv_norm, w_ukv, rope, need_ctx):
    def project(p):
        B, T, _ = p.shape
        c_q, c_kv, k_rope = jnp.split(p, [MLA_Q_RANK, MLA_Q_RANK + MLA_KV_RANK], axis=-1)
        q = (rms_norm(c_q, q_norm) @ w_uq).reshape(B, T, MLA_HEADS, MLA_NOPE + MLA_ROPE).transpose(0, 2, 1, 3)
        kv = (rms_norm(c_kv, kv_norm) @ w_ukv).reshape(B, T, MLA_HEADS, MLA_NOPE + MLA_V).transpose(0, 2, 1, 3)
        return q, kv[..., :MLA_NOPE], k_rope, kv[..., MLA_NOPE:]

    def full_keys(k_nope, k_rope):
        B, H, T, _ = k_nope.shape
        return jnp.concatenate([k_nope, jnp.broadcast_to(k_rope[:, None], (B, H, T, MLA_ROPE))], axis=-1)

    cos, sin = rope
    q_c, kn_c, kr_c, v_c = project(p_c)
    q_l, kn_l, kr_l, v_l = project(p_l)
    q_l = jnp.concatenate([q_l[..., :MLA_NOPE], apply_rope(q_l[..., MLA_NOPE:], cos, sin)], axis=-1)
    kr_l = apply_rope(kr_l, cos, sin)
    k_c = full_keys(kn_c, kr_c)
    k_l = full_keys(kn_l, kr_l)
    scale = (MLA_NOPE + MLA_ROPE) ** -0.5
    o_l = blocked_attention(q_l[:, :, None], jnp.concatenate([k_c, k_l], axis=2),
                            jnp.concatenate([v_c, v_l], axis=2), scale)
    o_c = merge_heads(blocked_attention(q_c[:, :, None], k_c, v_c, scale)) if need_ctx else None
    return o_c, merge_heads(o_l)


def gqa_mixer(p_c, p_l, q_norm, k_norm, rope, need_ctx):
    G = GQA_HEADS // GQA_KV_HEADS

    def project(p):
        B, T, _ = p.shape
        q, k, v = jnp.split(p, [GQA_HEADS * HEAD_DIM, (GQA_HEADS + GQA_KV_HEADS) * HEAD_DIM], axis=-1)
        q = rms_norm(q.reshape(B, T, GQA_KV_HEADS, G, HEAD_DIM), q_norm).transpose(0, 2, 3, 1, 4)
        k = rms_norm(k.reshape(B, T, GQA_KV_HEADS, HEAD_DIM), k_norm).transpose(0, 2, 1, 3)
        v = v.reshape(B, T, GQA_KV_HEADS, HEAD_DIM).transpose(0, 2, 1, 3)
        return q, k, v

    cos, sin = rope
    q_c, k_c, v_c = project(p_c)
    q_l, k_l, v_l = project(p_l)
    q_l = apply_rope(q_l, cos, sin)
    k_l = apply_rope(k_l, cos, sin)
    scale = HEAD_DIM ** -0.5
    o_l = blocked_attention(q_l, jnp.concatenate([k_c, k_l], axis=2), jnp.concatenate([v_c, v_l], axis=2), scale)
    o_c = merge_heads(blocked_attention(q_c, k_c, v_c, scale)) if need_ctx else None
    return o_c, merge_heads(o_l)


def gla_project(p, w_gf, b_gf, w_gb, b_gb):
    B, T, _ = p.shape
    nk = GLA_HEADS * GLA_DK
    nv = GLA_HEADS * GLA_DV
    q, k, v, g, r_f, r_b = jnp.split(
        p, [nk, 2 * nk, 2 * nk + nv, 2 * nk + 2 * nv, 2 * nk + 2 * nv + GLA_GATE_RANK], axis=-1)

    def heads(t, d):
        return t.reshape(B, T, GLA_HEADS, d).transpose(0, 2, 1, 3)

    def log_decay(r, w, b):
        return heads(jax.nn.log_sigmoid((r @ w + b).astype(jnp.float32)) / GLA_GATE_NORM, GLA_DK)

    return (heads(q, GLA_DK) * (GLA_DK ** -0.5), heads(k, GLA_DK), heads(v, GLA_DV), g,
            log_decay(r_f, w_gf, b_gf), log_decay(r_b, w_gb, b_gb))


def gla_scan(q, k, v, log_a, s0, reverse):
    if reverse:
        q, k, v, log_a = (jnp.flip(t, axis=2) for t in (q, k, v, log_a))
    B, H, T, dk = q.shape
    dv = v.shape[-1]
    n = T // GLA_CHUNK

    def chunks(t):
        return jnp.moveaxis(t.astype(jnp.float32).reshape(B, H, n, GLA_CHUNK, t.shape[-1]), 2, 0)

    order = jnp.tril(jnp.ones((GLA_CHUNK, GLA_CHUNK), dtype=bool))[:, :, None]

    def step(s, inp):
        qi, ki, vi, ai = inp
        b = jnp.cumsum(ai, axis=2)
        b_last = b[:, :, -1:, :]
        decay = jnp.exp(jnp.where(order, b[:, :, :, None, :] - b[:, :, None, :, :], -jnp.inf))
        scores = jnp.einsum('bhid,bhjd,bhijd->bhij', qi, ki, decay)
        o = scores @ vi + (qi * jnp.exp(b)) @ s
        s = jnp.exp(b_last[:, :, 0, :])[..., None] * s + jnp.swapaxes(ki * jnp.exp(b_last - b), -1, -2) @ vi
        return s, o

    s_final, o = lax.scan(step, s0, (chunks(q), chunks(k), chunks(v), chunks(log_a)))
    o = jnp.moveaxis(o, 0, 2).reshape(B, H, T, dv).astype(v.dtype)
    if reverse:
        o = jnp.flip(o, axis=2)
    return o, s_final


def gla_mixer(p_c, p_l, w_gf, b_gf, w_gb, b_gb, out_norm, need_ctx):
    q_c, k_c, v_c, g_c, af_c, ab_c = gla_project(p_c, w_gf, b_gf, w_gb, b_gb)
    q_l, k_l, v_l, g_l, af_l, ab_l = gla_project(p_l, w_gf, b_gf, w_gb, b_gb)
    s0 = jnp.zeros((p_l.shape[0], GLA_HEADS, GLA_DK, GLA_DV), jnp.float32)
    of_c, s_f = gla_scan(q_c, k_c, v_c, af_c, s0, False)
    ob_c, s_b = gla_scan(q_c, k_c, v_c, ab_c, s0, True)
    of_l, _ = gla_scan(q_l, k_l, v_l, af_l, s_f, False)
    ob_l, _ = gla_scan(q_l, k_l, v_l, ab_l, s_b, True)

    def finish(o, g):
        B, H, T, _ = o.shape
        o = rms_norm(o.transpose(0, 2, 1, 3), out_norm).reshape(B, T, H * GLA_DV)
        return o * jax.nn.silu(g)

    o_c = finish(of_c + ob_c, g_c) if need_ctx else None
    return o_c, finish(of_l + ob_l, g_l)


def mixer_sublayer(hc, h, w_in, mla_q_norm, mla_w_uq, mla_kv_norm, mla_w_ukv, gqa_q_norm, gqa_k_norm,
                   gla_w_gf, gla_b_gf, gla_w_gb, gla_b_gb, gla_out_norm, w_out, rope_gqa, rope_mla, need_ctx):
    n_ctx = hc.shape[1]
    p = jnp.concatenate([hc, h], axis=1) @ w_in
    p_c, p_l = p[:, :n_ctx], p[:, n_ctx:]
    a, b = MLA_COLS, MLA_COLS + GQA_COLS
    mla_c, mla_l = mla_mixer(p_c[..., :a], p_l[..., :a], mla_q_norm, mla_w_uq, mla_kv_norm, mla_w_ukv,
                             rope_mla, need_ctx)
    gqa_c, gqa_l = gqa_mixer(p_c[..., a:b], p_l[..., a:b], gqa_q_norm, gqa_k_norm, rope_gqa, need_ctx)
    gla_c, gla_l = gla_mixer(p_c[..., b:], p_l[..., b:], gla_w_gf, gla_b_gf, gla_w_gb, gla_b_gb,
                             gla_out_norm, need_ctx)
    y_l = jnp.concatenate([mla_l, gqa_l, gla_l], axis=-1)
    if need_ctx:
        y_c = jnp.concatenate([mla_c, gqa_c, gla_c], axis=-1)
        y = jnp.concatenate([y_c, y_l], axis=1) @ w_out
        return y[:, :n_ctx], y[:, n_ctx:]
    return None, y_l @ w_out


def hier_moe(h, w_rg, b_rg, w_re, b_re, w_g, w_u, w_d):
    B, T, D = h.shape
    t = h.reshape(B * T, D)
    grp_logits = (t @ w_rg + b_rg).astype(jnp.float32)
    grp_prob = jax.nn.softmax(grp_logits, axis=-1)
    grp = jnp.argmax(grp_logits, axis=-1)
    grp_w = jnp.take_along_axis(grp_prob, grp[:, None], axis=-1)
    exp_logits = (t @ w_re + b_re).astype(jnp.float32).reshape(-1, N_GROUPS, EXPERTS_PER_GROUP)
    in_grp = jnp.take_along_axis(exp_logits, grp[:, None, None], axis=1)[:, 0]
    top_logit, top_idx = lax.top_k(in_grp, TOP_K)
    gate = grp_w * jax.nn.softmax(top_logit, axis=-1)
    expert_id = grp[:, None] * EXPERTS_PER_GROUP + top_idx
    combine = jnp.einsum('nk,nke->ne', gate, jax.nn.one_hot(expert_id, N_EXPERTS, dtype=jnp.float32))
    out = None
    for gi in range(N_GROUPS):
        sl = slice(gi * EXPERTS_PER_GROUP, (gi + 1) * EXPERTS_PER_GROUP)
        act = jnp.einsum('nd,edf->nef', t, w_g[sl])
        up = jnp.einsum('nd,edf->nef', t, w_u[sl])
        hid = jax.nn.silu(act) * up * combine[:, sl, None].astype(t.dtype)
        part = jnp.einsum('nef,efd->nd', hid, w_d[sl])
        out = part if out is None else out + part
    return out.reshape(B, T, D).astype(h.dtype)


def setup_inputs(seed: int = 0) -> dict:
    key = jax.random.key(seed)
    ks = iter(jax.random.split(key, 32))
    L, D = DEPTH, D_MODEL

    def nrm(shape, scale):
        return jax.random.normal(next(ks), shape, jnp.float32) * scale

    def gain(shape):
        return 1.0 + nrm(shape, 0.02)

    return {
        "x": nrm((BATCH, SEQ, D), 1.0),
        "c": nrm((BATCH, D), 1.0),
        "ctx": nrm((BATCH, CTX_LEN, D), 1.0),
        "c_ctx": nrm((D,), 1.0),
        "ada_w": nrm((L, D, N_MOD * D), 0.5 * D ** -0.5),
        "ada_b": nrm((L, N_MOD * D), 0.02),
        "w_in": nrm((L, D, IN_COLS), D ** -0.5),
        "mla_q_norm": gain((L, MLA_Q_RANK)),
        "mla_w_uq": nrm((L, MLA_Q_RANK, MLA_HEADS * (MLA_NOPE + MLA_ROPE)), MLA_Q_RANK ** -0.5),
        "mla_kv_norm": gain((L, MLA_KV_RANK)),
        "mla_w_ukv": nrm((L, MLA_KV_RANK, MLA_HEADS * (MLA_NOPE + MLA_V)), MLA_KV_RANK ** -0.5),
        "gqa_q_norm": gain((L, HEAD_DIM)),
        "gqa_k_norm": gain((L, HEAD_DIM)),
        "gla_w_gate_fwd": nrm((L, GLA_GATE_RANK, GLA_HEADS * GLA_DK), GLA_GATE_RANK ** -0.5),
        "gla_b_gate_fwd": nrm((L, GLA_HEADS * GLA_DK), 0.02),
        "gla_w_gate_bwd": nrm((L, GLA_GATE_RANK, GLA_HEADS * GLA_DK), GLA_GATE_RANK ** -0.5),
        "gla_b_gate_bwd": nrm((L, GLA_HEADS * GLA_DK), 0.02),
        "gla_out_norm": gain((L, GLA_DV)),
        "w_out": nrm((L, MIX_WIDTH, D), MIX_WIDTH ** -0.5 * DEEPNORM_BETA),
        "ln1_g": gain((L, D)),
        "ln1_b": nrm((L, D), 0.02),
        "w_route_group": nrm((L, D, N_GROUPS), D ** -0.5),
        "b_route_group": nrm((L, N_GROUPS), 0.01),
        "w_route_expert": nrm((L, D, N_EXPERTS), D ** -0.5),
        "b_route_expert": nrm((L, N_EXPERTS), 0.01),
        "w_expert_gate": nrm((L, N_EXPERTS, D, EXPERT_HIDDEN), D ** -0.5),
        "w_expert_up": nrm((L, N_EXPERTS, D, EXPERT_HIDDEN), D ** -0.5),
        "w_expert_down": nrm((L, N_EXPERTS, EXPERT_HIDDEN, D), EXPERT_HIDDEN ** -0.5 * DEEPNORM_BETA),
        "ln2_g": gain((L, D)),
        "ln2_b": nrm((L, D), 0.02),
    }


def reference(x, c, ctx, c_ctx, ada_w, ada_b, w_in, mla_q_norm, mla_w_uq, mla_kv_norm, mla_w_ukv,
              gqa_q_norm, gqa_k_norm, gla_w_gate_fwd, gla_b_gate_fwd, gla_w_gate_bwd, gla_b_gate_bwd,
              gla_out_norm, w_out, ln1_g, ln1_b, w_route_group, b_route_group, w_route_expert,
              b_route_expert, w_expert_gate, w_expert_up, w_expert_down, ln2_g, ln2_b):
    n_lat = x.shape[1]
    rope_gqa = axial_rope_tables(n_lat, HEAD_DIM)
    rope_mla = axial_rope_tables(n_lat, MLA_ROPE)
    xc = ctx
    for l in range(DEPTH):
        need_ctx = l < DEPTH - 1
        mod = (jax.nn.silu(c) @ ada_w[l] + ada_b[l])[:, None, :]
        mod_c = (jax.nn.silu(c_ctx) @ ada_w[l] + ada_b[l])[None, None, :]
        sh1, sc1, g1, sh2, sc2, g2 = jnp.split(mod, N_MOD, axis=-1)
        sh1c, sc1c, g1c, sh2c, sc2c, g2c = jnp.split(mod_c, N_MOD, axis=-1)

        y_c, y_l = mixer_sublayer(
            xc * (1.0 + sc1c) + sh1c, x * (1.0 + sc1) + sh1, w_in[l], mla_q_norm[l], mla_w_uq[l],
            mla_kv_norm[l], mla_w_ukv[l], gqa_q_norm[l], gqa_k_norm[l], gla_w_gate_fwd[l],
            gla_b_gate_fwd[l], gla_w_gate_bwd[l], gla_b_gate_bwd[l], gla_out_norm[l], w_out[l],
            rope_gqa, rope_mla, need_ctx)
        x = layer_norm(DEEPNORM_ALPHA * x + g1 * y_l, ln1_g[l], ln1_b[l])

        moe_args = (w_route_group[l], b_route_group[l], w_route_expert[l], b_route_expert[l],
                    w_expert_gate[l], w_expert_up[l], w_expert_down[l])
        if need_ctx:
            xc = layer_norm(DEEPNORM_ALPHA * xc + g1c * y_c, ln1_g[l], ln1_b[l])
            n_ctx = xc.shape[1]
            f = hier_moe(jnp.concatenate([xc * (1.0 + sc2c) + sh2c, x * (1.0 + sc2) + sh2], axis=1), *moe_args)
            f_c, f_l = f[:, :n_ctx], f[:, n_ctx:]
            xc = layer_norm(DEEPNORM_ALPHA * xc + g2c * f_c, ln2_g[l], ln2_b[l])
        else:
            f_l = hier_moe(x * (1.0 + sc2) + sh2, *moe_args)
        x = layer_norm(DEEPNORM_ALPHA * x + g2 * f_l, ln2_g[l], ln2_b[l])
    return x
```

```python
import functools
import math

import jax
import jax.numpy as jnp
from jax import lax
from jax.experimental import pallas as pl
from jax.experimental.pallas import tpu as pltpu

GRID_W = 64
ROPE_THETA = 10000.0
NORM_EPS = 1e-6
LN_EPS = 1e-5
HEAD_DIM = 128
MLA_HEADS = 6
MLA_Q_RANK = 512
MLA_KV_RANK = 256
MLA_NOPE = 128
MLA_ROPE = 64
MLA_V = 128
GQA_HEADS = 6
GQA_KV_HEADS = 2
GQA_GROUP = GQA_HEADS // GQA_KV_HEADS
GLA_HEADS = 4
GLA_DK = 64
GLA_DV = 128
GLA_GATE_RANK = 16
GLA_GATE_NORM = 16.0
N_GROUPS = 4
EXPERTS_PER_GROUP = 8
N_EXPERTS = N_GROUPS * EXPERTS_PER_GROUP
N_MOD = 6

LANES = 128
VMEM_LIMIT = 48 * 1024 * 1024

ROW_TILE = 256
GLA_BLOCK = 16
MOE_TILE = 256

COL_CQ = 0
COL_CKV = 512
COL_GQ = 768
COL_GK = 1536
COL_GV = 1792
COL_LQ = 2048
COL_LK = 2304
COL_LV = 2560
COL_LG = 3072
COL_MISC = 3584
P_COLS = 3840
P_HALF = P_COLS // 2
MLA_QPAD = 256

LOG2E = math.log2(math.e)


def _cparams(sem):
    return pltpu.CompilerParams(dimension_semantics=sem, vmem_limit_bytes=VMEM_LIMIT)


def _dot(a, b):
    return jnp.dot(a, b, preferred_element_type=jnp.float32)


def _dot_nt(a, b):
    return lax.dot_general(a, b, (((1,), (1,)), ((), ())), preferred_element_type=jnp.float32)


def _mod_kernel(crep_ref, w_ref, b_ref, o_ref):
    tn = w_ref.shape[2]
    rows = []
    for r in range(2):
        cr = crep_ref[r]
        a = cr * jax.nn.sigmoid(cr)
        parts = []
        for j in range(tn // LANES):
            wj = w_ref[0, :, j * LANES:(j + 1) * LANES]
            parts.append(jnp.sum(wj * a, axis=0, keepdims=True))
        rows.append(jnp.concatenate(parts, axis=1) + b_ref[0])
    rid = lax.broadcasted_iota(jnp.int32, (8, tn), 0)
    o_ref[0] = jnp.where(rid == 0, rows[0], jnp.where(rid == 1, rows[1], 0.0))


def modulation(c, c_ctx, ada_w, ada_b):
    L, D, M = ada_w.shape
    tn = max(t for t in range(LANES, 1024 + 1, LANES) if M % t == 0)
    crep = jnp.broadcast_to(jnp.stack([c[0], c_ctx])[:, :, None], (2, D, LANES))
    return pl.pallas_call(
        _mod_kernel,
        grid=(L, M // tn),
        in_specs=[pl.BlockSpec((2, D, LANES), lambda l, j: (0, 0, 0)),
                  pl.BlockSpec((1, D, tn), lambda l, j: (l, 0, j)),
                  pl.BlockSpec((1, 1, tn), lambda l, j: (l, 0, j))],
        out_specs=pl.BlockSpec((1, 8, tn), lambda l, j: (l, 0, j)),
        out_shape=jax.ShapeDtypeStruct((L, 8, M), jnp.float32),
        compiler_params=_cparams(("arbitrary", "arbitrary")),
        name="modulation",
    )(crep, ada_w, ada_b.reshape(L, 1, M))


def _mod_row(ref, is_ctx):
    return jnp.where(is_ctx, ref[1:2, :], ref[0:1, :])


def _inproj_kernel(n_lat_tiles, x_ref, sh_ref, sc_ref, w_ref, o_ref):
    is_ctx = pl.program_id(1) >= n_lat_tiles
    h = x_ref[...] * (1.0 + _mod_row(sc_ref, is_ctx)) + _mod_row(sh_ref, is_ctx)
    o_ref[...] = _dot(h.astype(jnp.bfloat16), w_ref[...])


def in_projection(xs, mod_l, w_in_p, n_lat):
    N, D = xs.shape
    tm = ROW_TILE
    return pl.pallas_call(
        functools.partial(_inproj_kernel, n_lat // tm),
        grid=(2, N // tm),
        in_specs=[pl.BlockSpec((tm, D), lambda j, i: (i, 0)),
                  pl.BlockSpec((8, D), lambda j, i: (0, 0)),
                  pl.BlockSpec((8, D), lambda j, i: (0, 1)),
                  pl.BlockSpec((D, P_HALF), lambda j, i: (0, j))],
        out_specs=pl.BlockSpec((tm, P_HALF), lambda j, i: (i, j)),
        out_shape=jax.ShapeDtypeStruct((N, P_COLS), jnp.float32),
        compiler_params=_cparams(("arbitrary", "arbitrary")),
        name="in_projection",
    )(xs, mod_l, mod_l, w_in_p)


def _rope(x, cos, sin_a, sin_b, quarter):
    w = x.shape[-1]
    return x * cos + pltpu.roll(x, w - quarter, 1) * sin_a + pltpu.roll(x, quarter, 1) * sin_b


def _rms(x, gain):
    return x * lax.rsqrt(jnp.mean(x * x, axis=-1, keepdims=True) + NORM_EPS) * gain


def _mla_prep_kernel(cq_ref, ckv_ref, misc_ref, qn_ref, wuq_ref, kvn_ref, wukv_ref,
                     cos_ref, sa_ref, sb_ref, q_ref, k_ref, v_ref):
    cos, sa, sb = cos_ref[...], sa_ref[...], sb_ref[...]
    cq = _rms(cq_ref[...], qn_ref[...]).astype(jnp.bfloat16)
    q_all = _dot(cq, wuq_ref[...])
    ckv = _rms(ckv_ref[...], kvn_ref[...]).astype(jnp.bfloat16)
    kv_all = _dot(ckv, wukv_ref[...])
    lane = lax.broadcasted_iota(jnp.int32, misc_ref.shape, 1)
    k_rope = jnp.where(lane < MLA_ROPE, _rope(misc_ref[...], cos, sa, sb, MLA_ROPE // 4), 0.0)
    k_rope = k_rope.astype(jnp.bfloat16)
    for h in range(MLA_HEADS):
        b0 = h * MLA_QPAD
        q_ref[h, :, 0:LANES] = q_all[:, b0:b0 + LANES].astype(jnp.bfloat16)
        q_ref[h, :, LANES:2 * LANES] = _rope(q_all[:, b0 + LANES:b0 + 2 * LANES], cos, sa, sb,
                                             MLA_ROPE // 4).astype(jnp.bfloat16)
        k_ref[h, :, 0:LANES] = kv_all[:, b0:b0 + LANES].astype(jnp.bfloat16)
        k_ref[h, :, LANES:2 * LANES] = k_rope
        v_ref[h] = kv_all[:, b0 + LANES:b0 + 2 * LANES].astype(jnp.bfloat16)


def mla_prep(p, q_gain, w_uq_p, kv_gain, w_ukv, tabs):
    N = p.shape[0]
    tm = ROW_TILE
    H = MLA_HEADS
    row = lambda w, c: pl.BlockSpec((tm, w), lambda i: (i, c))
    full = lambda a: pl.BlockSpec(a.shape, lambda i: (0,) * a.ndim)
    tab = pl.BlockSpec((tm, LANES), lambda i: (i, 0))
    return pl.pallas_call(
        _mla_prep_kernel,
        grid=(N // tm,),
        in_specs=[row(MLA_Q_RANK, COL_CQ // MLA_Q_RANK), row(MLA_KV_RANK, COL_CKV // MLA_KV_RANK),
                  row(LANES, COL_MISC // LANES),
                  full(q_gain), full(w_uq_p), full(kv_gain), full(w_ukv), tab, tab, tab],
        out_specs=[pl.BlockSpec((H, tm, MLA_QPAD), lambda i: (0, i, 0)),
                   pl.BlockSpec((H, tm, MLA_QPAD), lambda i: (0, i, 0)),
                   pl.BlockSpec((H, tm, MLA_V), lambda i: (0, i, 0))],
        out_shape=[jax.ShapeDtypeStruct((H, N, MLA_QPAD), jnp.bfloat16),
                   jax.ShapeDtypeStruct((H, N, MLA_QPAD), jnp.bfloat16),
                   jax.ShapeDtypeStruct((H, N, MLA_V), jnp.bfloat16)],
        compiler_params=_cparams(("arbitrary",)),
        name="mla_prep",
    )(p, p, p, q_gain, w_uq_p, kv_gain, w_ukv, *tabs)


def _gqa_prep_kernel(q_in, k_in, v_in, qn_ref, kn_ref, cos_ref, sa_ref, sb_ref, q_ref, k_ref, v_ref):
    cos, sa, sb = cos_ref[...], sa_ref[...], sb_ref[...]
    qt = HEAD_DIM // 4
    for h in range(GQA_HEADS):
        x = _rms(q_in[:, h * HEAD_DIM:(h + 1) * HEAD_DIM], qn_ref[...])
        q_ref[h] = _rope(x, cos, sa, sb, qt).astype(jnp.bfloat16)
    for h in range(GQA_KV_HEADS):
        x = _rms(k_in[:, h * HEAD_DIM:(h + 1) * HEAD_DIM], kn_ref[...])
        k_ref[h] = _rope(x, cos, sa, sb, qt).astype(jnp.bfloat16)
        v_ref[h] = v_in[:, h * HEAD_DIM:(h + 1) * HEAD_DIM].astype(jnp.bfloat16)


def gqa_prep(p, q_gain, k_gain, tabs):
    N = p.shape[0]
    tm = ROW_TILE
    wq, wk = GQA_HEADS * HEAD_DIM, GQA_KV_HEADS * HEAD_DIM
    row = lambda w, c: pl.BlockSpec((tm, w), lambda i: (i, c))
    full = lambda a: pl.BlockSpec(a.shape, lambda i: (0,) * a.ndim)
    tab = pl.BlockSpec((tm, LANES), lambda i: (i, 0))
    return pl.pallas_call(
        _gqa_prep_kernel,
        grid=(N // tm,),
        in_specs=[row(wq, COL_GQ // wq), row(wk, COL_GK // wk), row(wk, COL_GV // wk),
                  full(q_gain), full(k_gain), tab, tab, tab],
        out_specs=[pl.BlockSpec((GQA_HEADS, tm, HEAD_DIM), lambda i: (0, i, 0)),
                   pl.BlockSpec((GQA_KV_HEADS, tm, HEAD_DIM), lambda i: (0, i, 0)),
                   pl.BlockSpec((GQA_KV_HEADS, tm, HEAD_DIM), lambda i: (0, i, 0))],
        out_shape=[jax.ShapeDtypeStruct((GQA_HEADS, N, HEAD_DIM), jnp.bfloat16),
                   jax.ShapeDtypeStruct((GQA_KV_HEADS, N, HEAD_DIM), jnp.bfloat16),
                   jax.ShapeDtypeStruct((GQA_KV_HEADS, N, HEAD_DIM), jnp.bfloat16)],
        compiler_params=_cparams(("arbitrary",)),
        name="gqa_prep",
    )(p, p, p, q_gain, k_gain, *tabs)


def _flash_kernel(group, q_ref, k_ref, v_ref, o_ref, m_ref, l_ref, acc_ref):
    ki = pl.program_id(2)
    tq, dv = q_ref.shape[1], v_ref.shape[2]

    @pl.when(ki == 0)
    def _():
        m_ref[...] = jnp.full(m_ref.shape, -jnp.inf, jnp.float32)
        l_ref[...] = jnp.zeros(l_ref.shape, jnp.float32)
        acc_ref[...] = jnp.zeros(acc_ref.shape, jnp.float32)

    q = q_ref[...].reshape(group * tq, q_ref.shape[2])
    s = _dot_nt(q, k_ref[0])
    m_prev = m_ref[...]
    m_new = jnp.maximum(m_prev, jnp.max(s, axis=1, keepdims=True))
    alpha = jnp.exp2(m_prev - m_new)
    p = jnp.exp2(s - m_new[:, 0:1])
    l_ref[...] = alpha * l_ref[...] + jnp.sum(p, axis=1, keepdims=True)
    acc_ref[...] = alpha * acc_ref[...] + _dot(p.astype(jnp.bfloat16), v_ref[0])
    m_ref[...] = m_new

    @pl.when(ki == pl.num_programs(2) - 1)
    def _():
        o = acc_ref[...] / l_ref[...]
        for g in range(group):
            o_ref[:, g * dv:(g + 1) * dv] = o[g * tq:(g + 1) * tq].astype(o_ref.dtype)


def flash_attention(q, k, v, *, group, tq, tk, q_row0, n_q, k_row0, n_k):
    hk, _, dq = k.shape
    dv = v.shape[2]
    assert dv == LANES and q_row0 % tq == 0 and k_row0 % tk == 0 and n_q % tq == 0 and n_k % tk == 0
    qb, kb = q_row0 // tq, k_row0 // tk
    m = group * tq
    return pl.pallas_call(
        functools.partial(_flash_kernel, group),
        grid=(hk, n_q // tq, n_k // tk),
        in_specs=[pl.BlockSpec((group, tq, dq), lambda h, i, j: (h, qb + i, 0)),
                  pl.BlockSpec((1, tk, dq), lambda h, i, j: (h, kb + j, 0)),
                  pl.BlockSpec((1, tk, dv), lambda h, i, j: (h, kb + j, 0))],
        out_specs=pl.BlockSpec((tq, group * dv), lambda h, i, j: (i, h)),
        out_shape=jax.ShapeDtypeStruct((n_q, hk * group * dv), jnp.bfloat16),
        scratch_shapes=[pltpu.VMEM((m, LANES), jnp.float32), pltpu.VMEM((m, LANES), jnp.float32),
                        pltpu.VMEM((m, dv), jnp.float32)],
        compiler_params=_cparams(("arbitrary", "arbitrary", "arbitrary")),
        name="flash_attention",
    )(q, k, v)


def _gla_block(q_ref, k_ref, v_ref, c_ref, o_ref, s_ref, ge_ref, bd_ref, r0, forward):
    nb = GLA_BLOCK
    rows = pl.ds(r0, nb)
    q, k, v, b = q_ref[rows, :], k_ref[rows, :], v_ref[rows, :], c_ref[rows, :]
    total = b[nb - 1:nb, :] if forward else b[0:1, :]
    st = s_ref[...]
    qe = (q * jnp.exp(b)).astype(jnp.bfloat16)
    o = _dot_nt(qe, st.astype(jnp.bfloat16))
    rid = lax.broadcasted_iota(jnp.int32, b.shape, 0)
    terms = []
    for j in range(nb):
        keep = (rid >= j) if forward else (rid <= j)
        w = jnp.exp(jnp.where(keep, b - b[j:j + 1, :], -jnp.inf))
        terms.append((q * w * k[j:j + 1, :]).astype(jnp.bfloat16))
    scores = _dot(jnp.concatenate(terms, axis=0), ge_ref[...])
    for j in range(nb):
        o = o + scores[j * nb:(j + 1) * nb, :] * v[j:j + 1, :]
    o_ref[rows, :] = o
    ke = (k * jnp.exp(total - b)).astype(jnp.bfloat16)
    upd = lax.dot_general(v.astype(jnp.bfloat16), ke, (((0,), (0,)), ((), ())),
                          preferred_element_type=jnp.float32)
    s_ref[...] = jnp.exp(total) * st + bd_ref[...] * upd


def _gla_kernel(qf, kf, vf, mf, qb, kb, vb, mb, wgf, bgf, wgb, bgb, tri_f, tri_b, ge, bd,
                of, ob, cf_ref, cb_ref, sf_ref, sb_ref):
    @pl.when(pl.program_id(0) == 0)
    def _():
        sf_ref[...] = jnp.zeros(sf_ref.shape, jnp.float32)
        sb_ref[...] = jnp.zeros(sb_ref.shape, jnp.float32)

    def cum_log_decay(m_ref, w_ref, b_ref, tri_ref):
        z = jnp.dot(m_ref[...], w_ref[...], precision=lax.Precision.HIGHEST,
                    preferred_element_type=jnp.float32) + b_ref[...]
        la = (jnp.minimum(z, 0.0) - jnp.log1p(jnp.exp(-jnp.abs(z)))) * (1.0 / GLA_GATE_NORM)
        return jnp.dot(tri_ref[...], la, precision=lax.Precision.HIGHEST,
                       preferred_element_type=jnp.float32)

    cf_ref[...] = cum_log_decay(mf, wgf, bgf, tri_f)
    cb_ref[...] = cum_log_decay(mb, wgb, bgb, tri_b)
    n_blocks = qf.shape[0] // GLA_BLOCK

    def body(t, carry):
        r_f = pl.multiple_of(t * GLA_BLOCK, GLA_BLOCK)
        r_b = pl.multiple_of((n_blocks - 1 - t) * GLA_BLOCK, GLA_BLOCK)
        _gla_block(qf, kf, vf, cf_ref, of, sf_ref, ge, bd, r_f, True)
        _gla_block(qb, kb, vb, cb_ref, ob, sb_ref, ge, bd, r_b, False)
        return carry

    lax.fori_loop(0, n_blocks, body, 0)


def gla_scan(p, w_gf_p, b_gf, w_gb_p, b_gb, consts):
    N = p.shape[0]
    tm = ROW_TILE
    nt = N // tm
    nk, nv = GLA_HEADS * GLA_DK, GLA_HEADS * GLA_DV
    fwd = lambda i: (i + nt - 1) % nt
    bwd = lambda i: (2 * nt - 1 - i) % nt
    col = lambda w, c, order: pl.BlockSpec((tm, w), lambda i: (order(i), c))
    full = lambda a: pl.BlockSpec(a.shape, lambda i: (0,) * a.ndim)
    side = lambda order: [col(nk, COL_LQ // nk, order), col(nk, COL_LK // nk, order),
                          col(nv, COL_LV // nv, order), col(LANES, COL_MISC // LANES, order)]
    small = (w_gf_p, b_gf, w_gb_p, b_gb) + tuple(consts)
    return pl.pallas_call(
        _gla_kernel,
        grid=(nt,),
        in_specs=side(fwd) + side(bwd) + [full(a) for a in small],
        out_specs=[pl.BlockSpec((tm, nv), lambda i: (fwd(i), 0)),
                   pl.BlockSpec((tm, nv), lambda i: (bwd(i), 0))],
        out_shape=[jax.ShapeDtypeStruct((N, nv), jnp.float32)] * 2,
        scratch_shapes=[pltpu.VMEM((tm, nk), jnp.float32), pltpu.VMEM((tm, nk), jnp.float32),
                        pltpu.VMEM((nv, nk), jnp.float32), pltpu.VMEM((nv, nk), jnp.float32)],
        compiler_params=_cparams(("arbitrary",)),
        name="gla_scan",
    )(p, p, p, p, p, p, p, p, *small)


def _gla_consts():
    i = jnp.arange(ROW_TILE)
    same = (i[:, None] // GLA_BLOCK) == (i[None, :] // GLA_BLOCK)
    tri_f = (same & (i[None, :] <= i[:, None])).astype(jnp.float32)
    tri_b = (same & (i[None, :] >= i[:, None])).astype(jnp.float32)
    hk = jnp.arange(GLA_HEADS * GLA_DK) // GLA_DK
    hv = jnp.arange(GLA_HEADS * GLA_DV) // GLA_DV
    ge = (hk[:, None] == hv[None, :]).astype(jnp.bfloat16)
    bd = (hv[:, None] == hk[None, :]).astype(jnp.float32)
    return tri_f, tri_b, ge, bd


def _prep_gate(w_f, b_f, w_b, b_b):
    z = jnp.zeros((LANES, w_f.shape[1]), jnp.float32)
    return (z.at[MLA_ROPE:MLA_ROPE + GLA_GATE_RANK].set(w_f), b_f[None],
            z.at[MLA_ROPE + GLA_GATE_RANK:MLA_ROPE + 2 * GLA_GATE_RANK].set(w_b), b_b[None])


def _layer_norm(u, g, b):
    mu = jnp.mean(u, axis=-1, keepdims=True)
    d = u - mu
    var = jnp.mean(d * d, axis=-1, keepdims=True)
    return d * lax.rsqrt(var + LN_EPS) * g + b


def _first_max(rows):
    best, idx = rows[0], jnp.zeros(rows[0].shape, jnp.int32)
    for i in range(1, len(rows)):
        better = rows[i] > best
        idx = jnp.where(better, i, idx)
        best = jnp.where(better, rows[i], best)
    return best, idx


def _route(lt):
    grp_rows = [lt[g:g + 1, :] for g in range(N_GROUPS)]
    g_best, grp = _first_max(grp_rows)
    grp_w = 1.0 / sum(jnp.exp(r - g_best) for r in grp_rows)
    in_grp = []
    for j in range(EXPERTS_PER_GROUP):
        val = lt[8 + j:9 + j, :]
        for g in range(1, N_GROUPS):
            r = 8 + g * EXPERTS_PER_GROUP + j
            val = jnp.where(grp == g, lt[r:r + 1, :], val)
        in_grp.append(val)
    v1, i1 = _first_max(in_grp)
    rest = [jnp.where(i1 == j, -jnp.inf, in_grp[j]) for j in range(EXPERTS_PER_GROUP)]
    v2, i2 = _first_max(rest)
    e = jnp.exp(v2 - v1)
    w1 = 1.0 / (1.0 + e)
    ids = jnp.concatenate([grp * EXPERTS_PER_GROUP + i1, grp * EXPERTS_PER_GROUP + i2], axis=0)
    gates = jnp.concatenate([grp_w * w1, grp_w * (e * w1)], axis=0)
    return ids, gates


def _outproj_kernel(n_lat_tiles, alpha,
                    x_ref, yml, ymc, ygl, ygc, of_ref, ob_ref, g_ref, on_ref, wo_ref,
                    g1_ref, sh2_ref, sc2_ref, lg_ref, lb_ref, wr_ref, br_ref,
                    x1_ref, h2_ref, id_ref, gt_ref):
    is_ctx = pl.program_id(0) >= n_lat_tiles
    y_mla = jnp.where(is_ctx, ymc[...], yml[...])
    y_gqa = jnp.where(is_ctx, ygc[...], ygl[...])
    o = of_ref[...] + ob_ref[...]
    heads = [_rms(o[:, h * GLA_DV:(h + 1) * GLA_DV], on_ref[...]) for h in range(GLA_HEADS)]
    g = g_ref[...]
    y_gla = (jnp.concatenate(heads, axis=1) * (g * jax.nn.sigmoid(g))).astype(jnp.bfloat16)
    w_mla, w_gqa = y_mla.shape[1], y_gqa.shape[1]
    y = (_dot(y_mla, wo_ref[0:w_mla, :]) + _dot(y_gqa, wo_ref[w_mla:w_mla + w_gqa, :])
         + _dot(y_gla, wo_ref[w_mla + w_gqa:, :]))
    x1 = _layer_norm(alpha * x_ref[...] + _mod_row(g1_ref, is_ctx) * y, lg_ref[...], lb_ref[...])
    x1_ref[...] = x1
    h2 = x1 * (1.0 + _mod_row(sc2_ref, is_ctx)) + _mod_row(sh2_ref, is_ctx)
    h2_ref[...] = h2
    lt = lax.dot_general(wr_ref[...], h2, (((1,), (1,)), ((), ())), precision=lax.Precision.HIGHEST,
                         preferred_element_type=jnp.float32) + br_ref[...]
    ids, gates = _route(lt)
    rid = lax.broadcasted_iota(jnp.int32, id_ref.shape, 0)
    id_ref[...] = jnp.where(rid == 0, ids[0:1], jnp.where(rid == 1, ids[1:2], 0))
    gt_ref[...] = jnp.where(rid == 0, gates[0:1], jnp.where(rid == 1, gates[1:2], 0.0))


def out_projection(xs, y_mla, y_gqa, o_f, o_b, p, out_norm, w_out, mod_l, ln_g, ln_b, w_rt, b_rt,
                   n_lat, alpha):
    N, D = xs.shape
    tm = ROW_TILE
    nl = n_lat // tm
    assert N - n_lat == tm
    nv = GLA_HEADS * GLA_DV
    lat = lambda a: pl.BlockSpec((tm, a.shape[1]), lambda i: (jnp.minimum(i, nl - 1), 0))
    ctx = lambda a: pl.BlockSpec((tm, a.shape[1]), lambda i: (0, 0))
    row = lambda w, c: pl.BlockSpec((tm, w), lambda i: (i, c))
    full = lambda a: pl.BlockSpec(a.shape, lambda i: (0,) * a.ndim)
    modc = lambda c: pl.BlockSpec((8, D), lambda i: (0, c))
    (yml, ymc), (ygl, ygc) = y_mla, y_gqa
    return pl.pallas_call(
        functools.partial(_outproj_kernel, nl, alpha),
        grid=(N // tm,),
        in_specs=[row(D, 0), lat(yml), ctx(ymc), lat(ygl), ctx(ygc), row(nv, 0), row(nv, 0),
                  row(nv, COL_LG // nv), full(out_norm), full(w_out),
                  modc(2), modc(3), modc(4), full(ln_g), full(ln_b), full(w_rt), full(b_rt)],
        out_specs=[row(D, 0), row(D, 0), pl.BlockSpec((8, tm), lambda i: (0, i)),
                   pl.BlockSpec((8, tm), lambda i: (0, i))],
        out_shape=[jax.ShapeDtypeStruct((N, D), jnp.float32), jax.ShapeDtypeStruct((N, D), jnp.float32),
                   jax.ShapeDtypeStruct((8, N), jnp.int32), jax.ShapeDtypeStruct((8, N), jnp.float32)],
        compiler_params=_cparams(("arbitrary",)),
        name="out_projection",
    )(xs, yml, ymc, ygl, ygc, o_f, o_b, p, out_norm, w_out, mod_l, mod_l, mod_l, ln_g, ln_b, w_rt, b_rt)


def _row_copy(src_hbm, idx_ref, dst, sem, r):
    return pltpu.make_async_copy(src_hbm.at[pl.ds(idx_ref[0, 0, r], 1), :], dst.at[pl.ds(r, 1), :], sem)


def _gather_rows(src_hbm, idx_ref, dst, sem, n):
    def start(r, c):
        _row_copy(src_hbm, idx_ref, dst, sem, r).start()
        return c

    def wait(r, c):
        _row_copy(src_hbm, idx_ref, dst, sem, r).wait()
        return c

    lax.fori_loop(0, n, start, 0)
    lax.fori_loop(0, n, wait, 0)


def _expert_kernel(te_ref, nu_ref, idx_ref, h_hbm, wg_ref, wu_ref, wd_ref, y_ref, xbuf, sem):
    i = pl.program_id(0)

    @pl.when(i < nu_ref[0])
    def _():
        _gather_rows(h_hbm, idx_ref, xbuf, sem, xbuf.shape[0])
        x = xbuf[...].astype(jnp.bfloat16)
        a = _dot(x, wg_ref[0])
        u = _dot(x, wu_ref[0])
        hid = (a * jax.nn.sigmoid(a) * u).astype(jnp.bfloat16)
        y_ref[...] = _dot(hid, wd_ref[0])

    @pl.when(i >= nu_ref[0])
    def _():
        y_ref[...] = jnp.zeros(y_ref.shape, jnp.float32)


def expert_ffn(h2, tile_expert, n_used, sorted_tok, w_g, w_u, w_d):
    N, D = h2.shape
    tm = MOE_TILE
    n_tiles = sorted_tok.shape[0] // tm
    E, _, F = w_g.shape
    gs = pltpu.PrefetchScalarGridSpec(
        num_scalar_prefetch=2,
        grid=(n_tiles,),
        in_specs=[pl.BlockSpec((1, 1, tm), lambda i, te, nu: (i, 0, 0), memory_space=pltpu.SMEM),
                  pl.BlockSpec(memory_space=pl.ANY),
                  pl.BlockSpec((1, D, F), lambda i, te, nu: (te[i], 0, 0)),
                  pl.BlockSpec((1, D, F), lambda i, te, nu: (te[i], 0, 0)),
                  pl.BlockSpec((1, F, D), lambda i, te, nu: (te[i], 0, 0))],
        out_specs=pl.BlockSpec((tm, D), lambda i, te, nu: (i, 0)),
        scratch_shapes=[pltpu.VMEM((tm, D), jnp.float32), pltpu.SemaphoreType.DMA(())],
    )
    return pl.pallas_call(
        _expert_kernel,
        grid_spec=gs,
        out_shape=jax.ShapeDtypeStruct((n_tiles * tm, D), jnp.float32),
        compiler_params=_cparams(("arbitrary",)),
        name="expert_ffn",
    )(tile_expert, n_used, sorted_tok.reshape(n_tiles, 1, tm), h2, w_g, w_u, w_d)


def _combine_kernel(n_lat_tiles, alpha, pos_ref, y_hbm, x_ref, gc_ref, g2_ref, lg_ref, lb_ref,
                    o_ref, ybuf, sem):
    is_ctx = pl.program_id(0) >= n_lat_tiles
    tm = x_ref.shape[0]
    _gather_rows(y_hbm, pos_ref, ybuf, sem, 2 * tm)
    gc = gc_ref[...]
    f = gc[:, 0:1] * ybuf[0:tm, :] + gc[:, 1:2] * ybuf[tm:2 * tm, :]
    o_ref[...] = _layer_norm(alpha * x_ref[...] + _mod_row(g2_ref, is_ctx) * f, lg_ref[...], lb_ref[...])


def moe_combine(xs1, y_sorted, pos, gate_cols, mod_l, ln_g, ln_b, n_lat, alpha):
    N, D = xs1.shape
    tm = ROW_TILE
    full = lambda a: pl.BlockSpec(a.shape, lambda i: (0,) * a.ndim)
    return pl.pallas_call(
        functools.partial(_combine_kernel, n_lat // tm, alpha),
        grid=(N // tm,),
        in_specs=[pl.BlockSpec((1, 1, 2 * tm), lambda i: (i, 0, 0), memory_space=pltpu.SMEM),
                  pl.BlockSpec(memory_space=pl.ANY),
                  pl.BlockSpec((tm, D), lambda i: (i, 0)),
                  pl.BlockSpec((tm, 2), lambda i: (i, 0)),
                  pl.BlockSpec((8, D), lambda i: (0, 5)), full(ln_g), full(ln_b)],
        out_specs=pl.BlockSpec((tm, D), lambda i: (i, 0)),
        out_shape=jax.ShapeDtypeStruct((N, D), jnp.float32),
        scratch_shapes=[pltpu.VMEM((2 * tm, D), jnp.float32), pltpu.SemaphoreType.DMA(())],
        compiler_params=_cparams(("arbitrary",)),
        name="moe_combine",
    )(pos, y_sorted, xs1, gate_cols, mod_l, ln_g, ln_b)


def _dispatch_plan(ids, n_tok):
    tm = MOE_TILE
    n_asg = 2 * n_tok
    n_tiles = n_asg // tm + N_EXPERTS
    e_flat = ids.reshape(n_asg)
    tok_flat = jnp.tile(jnp.arange(n_tok, dtype=jnp.int32), 2)
    order = jnp.argsort(e_flat, stable=True).astype(jnp.int32)
    sorted_e = e_flat[order]
    experts = jnp.arange(N_EXPERTS, dtype=jnp.int32)
    starts = jnp.searchsorted(sorted_e, experts, side="left").astype(jnp.int32)
    ends = jnp.searchsorted(sorted_e, experts, side="right").astype(jnp.int32)
    padded = ((ends - starts + tm - 1) // tm) * tm
    pad_ends = jnp.cumsum(padded).astype(jnp.int32)
    pad_starts = pad_ends - padded
    dest = pad_starts[sorted_e] + (jnp.arange(n_asg, dtype=jnp.int32) - starts[sorted_e])
    sorted_tok = jnp.zeros((n_tiles * tm,), jnp.int32).at[dest].set(tok_flat[order])
    pos = jnp.zeros((n_asg,), jnp.int32).at[order].set(dest)
    n_used = (pad_ends[-1] // tm).reshape(1)
    tile_start = jnp.arange(n_tiles, dtype=jnp.int32) * tm
    tile_expert = jnp.searchsorted(pad_ends, tile_start, side="right").astype(jnp.int32)
    last = jnp.searchsorted(pad_ends, pad_ends[-1] - 1, side="right").astype(jnp.int32)
    tile_expert = jnp.minimum(tile_expert, last)
    nt = n_tok // ROW_TILE
    pos_tiles = jnp.concatenate([pos[:n_tok].reshape(nt, 1, ROW_TILE),
                                 pos[n_tok:].reshape(nt, 1, ROW_TILE)], axis=2)
    return sorted_tok, tile_expert, n_used, pos_tiles


def _rope_tables(n_lat, n_ctx, dim):
    rows = n_lat // GRID_W
    row = jnp.repeat(jnp.arange(rows), GRID_W).astype(jnp.float32)
    col = jnp.tile(jnp.arange(GRID_W), rows).astype(jnp.float32)
    half = dim // 2
    inv_freq = ROPE_THETA ** (-jnp.arange(0, half, 2, dtype=jnp.float32) / half)
    ang_r = row[:, None] * inv_freq
    ang_c = col[:, None] * inv_freq
    ang = jnp.concatenate([ang_r, ang_r, ang_c, ang_c], axis=-1)
    cos, sin = jnp.cos(ang), jnp.sin(ang)
    q = dim // 4
    lane = jnp.arange(dim)
    first = ((lane // q) % 2) == 0
    sin_a = jnp.where(first, -sin, 0.0)
    sin_b = jnp.where(first, 0.0, sin)

    def pad(t, fill):
        t = jnp.pad(t, ((0, 0), (0, LANES - dim)), constant_values=fill)
        return jnp.pad(t, ((0, n_ctx), (0, 0)), constant_values=fill)

    return pad(cos, 1.0), pad(sin_a, 0.0), pad(sin_b, 0.0)


def _in_perm():
    a = MLA_Q_RANK + MLA_KV_RANK + MLA_ROPE
    b = a + (GQA_HEADS + 2 * GQA_KV_HEADS) * HEAD_DIM
    nk, nv = GLA_HEADS * GLA_DK, GLA_HEADS * GLA_DV
    r = lambda s, n: list(range(s, s + n))
    cols = (r(0, MLA_Q_RANK + MLA_KV_RANK)
            + r(a, b - a)
            + r(b, 2 * nk + 2 * nv)
            + r(MLA_Q_RANK + MLA_KV_RANK, MLA_ROPE)
            + r(b + 2 * nk + 2 * nv, 2 * GLA_GATE_RANK))
    return jnp.asarray(cols, jnp.int32)


def _prep_w_in(w_in_l):
    w = w_in_l[:, _in_perm()]
    scale = jnp.ones((w.shape[1],), jnp.float32).at[COL_LQ:COL_LK].set(GLA_DK ** -0.5)
    w = w * scale
    return jnp.pad(w, ((0, 0), (0, P_COLS - w.shape[1]))).astype(jnp.bfloat16)


def _prep_w_uq(w_uq_l):
    r = w_uq_l.shape[0]
    w = w_uq_l.reshape(r, MLA_HEADS, MLA_NOPE + MLA_ROPE)
    w = jnp.pad(w, ((0, 0), (0, 0), (0, MLA_QPAD - MLA_NOPE - MLA_ROPE)))
    return w.reshape(r, MLA_HEADS * MLA_QPAD).astype(jnp.bfloat16)


def _prep_router(w_rg, b_rg, w_re, b_re):
    D = w_rg.shape[0]
    w = jnp.zeros((LANES, D), jnp.float32).at[0:N_GROUPS].set(w_rg.T).at[8:8 + N_EXPERTS].set(w_re.T)
    b = jnp.zeros((LANES,), jnp.float32).at[0:N_GROUPS].set(b_rg).at[8:8 + N_EXPERTS].set(b_re)
    return w, b[:, None]


def _kv_tile(n):
    return max(t for t in (256, 640, 1280) if n % t == 0)


def kernel(x, c, ctx, c_ctx, ada_w, ada_b, w_in, mla_q_norm, mla_w_uq, mla_kv_norm, mla_w_ukv, gqa_q_norm, gqa_k_norm, gla_w_gate_fwd, gla_b_gate_fwd, gla_w_gate_bwd, gla_b_gate_bwd, gla_out_norm, w_out, ln1_g, ln1_b, w_route_group, b_route_group, w_route_expert, b_route_expert, w_expert_gate, w_expert_up, w_expert_down, ln2_g, ln2_b):
    T, C = x.shape[1], ctx.shape[1]
    N = T + C
    L = ada_w.shape[0]
    assert x.shape[0] == 1 and C == ROW_TILE and T % ROW_TILE == 0 and T % GRID_W == 0
    alpha = (2.0 * L) ** 0.25
    bf = jnp.bfloat16

    xs = jnp.concatenate([x[0], ctx[0]], axis=0)
    mods = modulation(c, c_ctx, ada_w, ada_b)
    tabs_mla = _rope_tables(T, C, MLA_ROPE)
    tabs_gqa = _rope_tables(T, C, HEAD_DIM)
    gla_consts = _gla_consts()
    mla_scale = (MLA_NOPE + MLA_ROPE) ** -0.5 * LOG2E
    gqa_scale = HEAD_DIM ** -0.5 * LOG2E
    tk = _kv_tile(N)
    tq_mla = 512 if T % 512 == 0 else ROW_TILE
    lat = dict(q_row0=0, n_q=T, k_row0=0, n_k=N, tk=tk)
    con = dict(q_row0=T, n_q=C, k_row0=T, n_k=C, tq=C, tk=C)

    for l in range(L):
        mod_l = mods[l]
        p = in_projection(xs, mod_l, _prep_w_in(w_in[l]), T)

        q, k, v = mla_prep(p, (mla_q_norm[l] * mla_scale)[None], _prep_w_uq(mla_w_uq[l]),
                           mla_kv_norm[l][None], mla_w_ukv[l].astype(bf), tabs_mla)
        y_mla = (flash_attention(q, k, v, group=1, tq=tq_mla, **lat),
                 flash_attention(q, k, v, group=1, **con))
        q, k, v = gqa_prep(p, (gqa_q_norm[l] * gqa_scale)[None], gqa_k_norm[l][None], tabs_gqa)
        y_gqa = (flash_attention(q, k, v, group=GQA_GROUP, tq=ROW_TILE, **lat),
                 flash_attention(q, k, v, group=GQA_GROUP, **con))
        o_f, o_b = gla_scan(p, *_prep_gate(gla_w_gate_fwd[l], gla_b_gate_fwd[l],
                                           gla_w_gate_bwd[l], gla_b_gate_bwd[l]), gla_consts)

        w_rt, b_rt = _prep_router(w_route_group[l], b_route_group[l], w_route_expert[l], b_route_expert[l])
        x1, h2, ids, gates = out_projection(xs, y_mla, y_gqa, o_f, o_b, p, gla_out_norm[l][None],
                                            w_out[l].astype(bf), mod_l, ln1_g[l][None], ln1_b[l][None],
                                            w_rt, b_rt, T, alpha)

        sorted_tok, tile_expert, n_used, pos_tiles = _dispatch_plan(ids[0:2], N)
        y_sorted = expert_ffn(h2, tile_expert, n_used, sorted_tok, w_expert_gate[l].astype(bf),
                              w_expert_up[l].astype(bf), w_expert_down[l].astype(bf))
        xs = moe_combine(x1, y_sorted, pos_tiles, gates[0:2].T, mod_l, ln2_g[l][None], ln2_b[l][None],
                         T, alpha)
    return xs[:T][None]
```

```python
import functools
import math

import jax
import jax.numpy as jnp
from jax import lax
from jax.experimental import pallas as pl
from jax.experimental.pallas import tpu as pltpu

GRID_W = 64
ROPE_THETA = 10000.0
NORM_EPS = 1e-6
LN_EPS = 1e-5
HEAD_DIM = 128
MLA_HEADS = 6
MLA_Q_RANK = 512
MLA_KV_RANK = 256
MLA_NOPE = 128
MLA_ROPE = 64
MLA_V = 128
GQA_HEADS = 6
GQA_KV_HEADS = 2
GQA_GROUP = GQA_HEADS // GQA_KV_HEADS
GLA_HEADS = 4
GLA_DK = 64
GLA_DV = 128
GLA_GATE_RANK = 16
GLA_GATE_NORM = 16.0
N_GROUPS = 4
EXPERTS_PER_GROUP = 8
N_EXPERTS = N_GROUPS * EXPERTS_PER_GROUP
N_MOD = 6

LANES = 128
VMEM_LIMIT = 48 * 1024 * 1024

ROW_TILE = 256
GLA_BLOCK = 16
MOE_TILE = 256

COL_CQ = 0
COL_CKV = 512
COL_GQ = 768
COL_GK = 1536
COL_GV = 1792
COL_LQ = 2048
COL_LK = 2304
COL_LV = 2560
COL_LG = 3072
COL_MISC = 3584
P_COLS = 3840
P_HALF = P_COLS // 2
MLA_QPAD = 256

LOG2E = math.log2(math.e)


def _cparams(sem):
    return pltpu.CompilerParams(dimension_semantics=sem, vmem_limit_bytes=VMEM_LIMIT)


def _dot(a, b):
    return jnp.dot(a, b, preferred_element_type=jnp.float32)


def _dot_nt(a, b):
    return lax.dot_general(a, b, (((1,), (1,)), ((), ())), preferred_element_type=jnp.float32)


def _mod_kernel(crep_ref, w_ref, b_ref, o_ref):
    tn = w_ref.shape[2]
    rows = []
    for r in range(2):
        cr = crep_ref[r]
        a = cr * jax.nn.sigmoid(cr)
        parts = []
        for j in range(tn // LANES):
            wj = w_ref[0, :, j * LANES:(j + 1) * LANES]
            parts.append(jnp.sum(wj * a, axis=0, keepdims=True))
        rows.append(jnp.concatenate(parts, axis=1) + b_ref[0])
    rid = lax.broadcasted_iota(jnp.int32, (8, tn), 0)
    o_ref[0] = jnp.where(rid == 0, rows[0], jnp.where(rid == 1, rows[1], 0.0))


def modulation(c, c_ctx, ada_w, ada_b):
    L, D, M = ada_w.shape
    tn = max(t for t in range(LANES, 1024 + 1, LANES) if M % t == 0)
    crep = jnp.broadcast_to(jnp.stack([c[0], c_ctx])[:, :, None], (2, D, LANES))
    return pl.pallas_call(
        _mod_kernel,
        grid=(L, M // tn),
        in_specs=[pl.BlockSpec((2, D, LANES), lambda l, j: (0, 0, 0)),
                  pl.BlockSpec((1, D, tn), lambda l, j: (l, 0, j)),
                  pl.BlockSpec((1, 1, tn), lambda l, j: (l, 0, j))],
        out_specs=pl.BlockSpec((1, 8, tn), lambda l, j: (l, 0, j)),
        out_shape=jax.ShapeDtypeStruct((L, 8, M), jnp.float32),
        compiler_params=_cparams(("arbitrary", "arbitrary")),
        name="modulation",
    )(crep, ada_w, ada_b.reshape(L, 1, M))


def _mod_row(ref, is_ctx):
    return jnp.where(is_ctx, ref[1:2, :], ref[0:1, :])


def _inproj_kernel(n_lat_tiles, x_ref, sh_ref, sc_ref, w_ref, o_ref):
    is_ctx = pl.program_id(1) >= n_lat_tiles
    h = x_ref[...] * (1.0 + _mod_row(sc_ref, is_ctx)) + _mod_row(sh_ref, is_ctx)
    o_ref[...] = _dot(h.astype(jnp.bfloat16), w_ref[...])


def in_projection(xs, mod_l, w_in_p, n_lat):
    N, D = xs.shape
    tm = ROW_TILE
    return pl.pallas_call(
        functools.partial(_inproj_kernel, n_lat // tm),
        grid=(2, N // tm),
        in_specs=[pl.BlockSpec((tm, D), lambda j, i: (i, 0)),
                  pl.BlockSpec((8, D), lambda j, i: (0, 0)),
                  pl.BlockSpec((8, D), lambda j, i: (0, 1)),
                  pl.BlockSpec((D, P_HALF), lambda j, i: (0, j))],
        out_specs=pl.BlockSpec((tm, P_HALF), lambda j, i: (i, j)),
        out_shape=jax.ShapeDtypeStruct((N, P_COLS), jnp.float32),
        compiler_params=_cparams(("arbitrary", "arbitrary")),
        name="in_projection",
    )(xs, mod_l, mod_l, w_in_p)


def _rope(x, cos, sin_a, sin_b, quarter):
    w = x.shape[-1]
    return x * cos + pltpu.roll(x, w - quarter, 1) * sin_a + pltpu.roll(x, quarter, 1) * sin_b


def _rms(x, gain):
    return x * lax.rsqrt(jnp.mean(x * x, axis=-1, keepdims=True) + NORM_EPS) * gain


def _rope_t(x, cos, sin_a, sin_b, quarter):
    w = x.shape[0]
    return x * cos + pltpu.roll(x, w - quarter, 0) * sin_a + pltpu.roll(x, quarter, 0) * sin_b


def _mla_prep_kernel(cq_ref, ckv_ref, misc_ref, qn_ref, wuqt_ref, kvn_ref, wkn_ref, wvt_ref,
                     cos_ref, sa_ref, sb_ref, cost_ref, sat_ref, sbt_ref, qt_ref, k_ref, vt_ref):
    cq = _rms(cq_ref[...], qn_ref[...]).astype(jnp.bfloat16)
    qt_all = _dot_nt(wuqt_ref[...], cq)
    ckv = _rms(ckv_ref[...], kvn_ref[...]).astype(jnp.bfloat16)
    kn_all = _dot(ckv, wkn_ref[...])
    vt_all = _dot_nt(wvt_ref[...], ckv)
    lane = lax.broadcasted_iota(jnp.int32, misc_ref.shape, 1)
    k_rope = _rope(misc_ref[...], cos_ref[...], sa_ref[...], sb_ref[...], MLA_ROPE // 4)
    k_rope = jnp.where(lane < MLA_ROPE, k_rope, 0.0).astype(jnp.bfloat16)
    cost, sat, sbt = cost_ref[...], sat_ref[...], sbt_ref[...]
    for h in range(MLA_HEADS):
        b0 = h * MLA_QPAD
        qt_ref[h, 0:LANES, :] = qt_all[b0:b0 + LANES, :].astype(jnp.bfloat16)
        qt_ref[h, LANES:2 * LANES, :] = _rope_t(qt_all[b0 + LANES:b0 + 2 * LANES, :], cost, sat, sbt,
                                                MLA_ROPE // 4).astype(jnp.bfloat16)
        k_ref[h, :, 0:LANES] = kn_all[:, h * MLA_NOPE:(h + 1) * MLA_NOPE].astype(jnp.bfloat16)
        k_ref[h, :, LANES:2 * LANES] = k_rope
        vt_ref[h] = vt_all[h * MLA_V:(h + 1) * MLA_V, :].astype(jnp.bfloat16)


def mla_prep(p, q_gain, w_uq_t, kv_gain, w_kn, w_v_t, tabs, tabs_t):
    N = p.shape[0]
    tm = ROW_TILE
    H = MLA_HEADS
    row = lambda w, c: pl.BlockSpec((tm, w), lambda i: (i, c))
    full = lambda a: pl.BlockSpec(a.shape, lambda i: (0,) * a.ndim)
    tab = pl.BlockSpec((tm, LANES), lambda i: (i, 0))
    tab_t = pl.BlockSpec((LANES, tm), lambda i: (0, i))
    return pl.pallas_call(
        _mla_prep_kernel,
        grid=(N // tm,),
        in_specs=[row(MLA_Q_RANK, COL_CQ // MLA_Q_RANK), row(MLA_KV_RANK, COL_CKV // MLA_KV_RANK),
                  row(LANES, COL_MISC // LANES),
                  full(q_gain), full(w_uq_t), full(kv_gain), full(w_kn), full(w_v_t),
                  tab, tab, tab, tab_t, tab_t, tab_t],
        out_specs=[pl.BlockSpec((H, MLA_QPAD, tm), lambda i: (0, 0, i)),
                   pl.BlockSpec((H, tm, MLA_QPAD), lambda i: (0, i, 0)),
                   pl.BlockSpec((H, MLA_V, tm), lambda i: (0, 0, i))],
        out_shape=[jax.ShapeDtypeStruct((H, MLA_QPAD, N), jnp.bfloat16),
                   jax.ShapeDtypeStruct((H, N, MLA_QPAD), jnp.bfloat16),
                   jax.ShapeDtypeStruct((H, MLA_V, N), jnp.bfloat16)],
        compiler_params=_cparams(("arbitrary",)),
        name="mla_prep",
    )(p, p, p, q_gain, w_uq_t, kv_gain, w_kn, w_v_t, *tabs, *tabs_t)


def _gqa_prep_kernel(q_in, k_in, v_in, qn_ref, kn_ref, cos_ref, sa_ref, sb_ref,
                     cost_ref, sat_ref, sbt_ref, qt_ref, k_ref, vt_ref):
    cos, sa, sb = cos_ref[...], sa_ref[...], sb_ref[...]
    cost, sat, sbt = cost_ref[...], sat_ref[...], sbt_ref[...]
    qt = HEAD_DIM // 4
    for h in range(GQA_HEADS):
        xt = q_in[:, h * HEAD_DIM:(h + 1) * HEAD_DIM].T
        xt = xt * lax.rsqrt(jnp.mean(xt * xt, axis=0, keepdims=True) + NORM_EPS) * qn_ref[...]
        qt_ref[h] = _rope_t(xt, cost, sat, sbt, qt).astype(jnp.bfloat16)
    for h in range(GQA_KV_HEADS):
        x = _rms(k_in[:, h * HEAD_DIM:(h + 1) * HEAD_DIM], kn_ref[...])
        k_ref[h] = _rope(x, cos, sa, sb, qt).astype(jnp.bfloat16)
        vt_ref[h] = v_in[:, h * HEAD_DIM:(h + 1) * HEAD_DIM].T.astype(jnp.bfloat16)


def gqa_prep(p, q_gain_col, k_gain, tabs, tabs_t):
    N = p.shape[0]
    tm = ROW_TILE
    wq, wk = GQA_HEADS * HEAD_DIM, GQA_KV_HEADS * HEAD_DIM
    row = lambda w, c: pl.BlockSpec((tm, w), lambda i: (i, c))
    full = lambda a: pl.BlockSpec(a.shape, lambda i: (0,) * a.ndim)
    tab = pl.BlockSpec((tm, LANES), lambda i: (i, 0))
    tab_t = pl.BlockSpec((LANES, tm), lambda i: (0, i))
    return pl.pallas_call(
        _gqa_prep_kernel,
        grid=(N // tm,),
        in_specs=[row(wq, COL_GQ // wq), row(wk, COL_GK // wk), row(wk, COL_GV // wk),
                  full(q_gain_col), full(k_gain), tab, tab, tab, tab_t, tab_t, tab_t],
        out_specs=[pl.BlockSpec((GQA_HEADS, HEAD_DIM, tm), lambda i: (0, 0, i)),
                   pl.BlockSpec((GQA_KV_HEADS, tm, HEAD_DIM), lambda i: (0, i, 0)),
                   pl.BlockSpec((GQA_KV_HEADS, HEAD_DIM, tm), lambda i: (0, 0, i))],
        out_shape=[jax.ShapeDtypeStruct((GQA_HEADS, HEAD_DIM, N), jnp.bfloat16),
                   jax.ShapeDtypeStruct((GQA_KV_HEADS, N, HEAD_DIM), jnp.bfloat16),
                   jax.ShapeDtypeStruct((GQA_KV_HEADS, HEAD_DIM, N), jnp.bfloat16)],
        compiler_params=_cparams(("arbitrary",)),
        name="gqa_prep",
    )(p, p, p, q_gain_col, k_gain, *tabs, *tabs_t)


KEY_CHUNK = 256


def _sublane_fold(x, op):
    return op(x.reshape(x.shape[0] // 8, 8, x.shape[1]), axis=0)


def _flash_kernel(group, q_tiles, tks, qt_ref, k_ref, vt_ref, o_ref, s_even, s_odd, m_ref, l_ref, acc_ref):
    dv = vt_ref.shape[1]
    tq = ROW_TILE
    n_chunks = tks // KEY_CHUNK
    n_super = k_ref.shape[1] // tks
    streams = [(g, t) for g in range(group) for t in range(q_tiles)]
    m_ref[...] = jnp.full(m_ref.shape, -jnp.inf, jnp.float32)
    l_ref[...] = jnp.zeros(l_ref.shape, jnp.float32)
    acc_ref[...] = jnp.zeros(acc_ref.shape, jnp.float32)

    def stage(j_a, buf_a, j_b, buf_b, cmax_b):
        if j_a is not None:
            ka = pl.multiple_of(j_a * tks, tks)
            cmax_a = [None] * len(streams)
        if j_b is not None:
            kb = pl.multiple_of(j_b * tks, tks)
            m_old = [m_ref[s] for s in range(len(streams))]
            m_new = [jnp.maximum(m_old[s], jnp.max(cmax_b[s], axis=0, keepdims=True))
                     for s in range(len(streams))]
            lsum = [jnp.zeros((8, tq), jnp.float32) for _ in streams]
            pv = [None] * len(streams)
        for c in range(n_chunks):
            rows = slice(c * KEY_CHUNK, (c + 1) * KEY_CHUNK)
            if j_a is not None:
                k_c = k_ref[0, pl.ds(ka + c * KEY_CHUNK, KEY_CHUNK), :]
            if j_b is not None:
                vt_c = vt_ref[0, :, pl.ds(kb + c * KEY_CHUNK, KEY_CHUNK)]
            for s, (g, t) in enumerate(streams):
                if j_a is not None:
                    sc = _dot(k_c, qt_ref[g, :, t * tq:(t + 1) * tq])
                    buf_a[s, rows, :] = sc
                    cm = _sublane_fold(sc, jnp.max)
                    cmax_a[s] = cm if cmax_a[s] is None else jnp.maximum(cmax_a[s], cm)
                if j_b is not None:
                    p = jnp.exp2(buf_b[s, rows, :] - m_new[s])
                    lsum[s] = lsum[s] + _sublane_fold(p, jnp.sum)
                    d = _dot(vt_c, p.astype(jnp.bfloat16))
                    pv[s] = d if pv[s] is None else d + pv[s]
        if j_b is not None:
            for s in range(len(streams)):
                alpha = jnp.exp2(m_old[s] - m_new[s])
                l_ref[s] = alpha * l_ref[s] + jnp.sum(lsum[s], axis=0, keepdims=True)
                acc_ref[s] = alpha * acc_ref[s] + pv[s]
                m_ref[s] = m_new[s]
        return tuple(cmax_a) if j_a is not None else None

    def pair(i, cm):
        j = 2 * i + 1
        cm = stage(j, s_odd, j - 1, s_even, cm)
        return stage(j + 1, s_even, j, s_odd, cm)

    cmax = stage(0, s_even, None, None, None)
    cmax = lax.fori_loop(0, (n_super - 1) // 2, pair, cmax)
    if n_super % 2 == 0:
        cmax = stage(n_super - 1, s_odd, n_super - 2, s_even, cmax)
    stage(None, None, n_super - 1, s_odd if n_super % 2 == 0 else s_even, cmax)
    for s, (g, t) in enumerate(streams):
        o_ref[t * tq:(t + 1) * tq, g * dv:(g + 1) * dv] = (acc_ref[s] / l_ref[s]).T.astype(o_ref.dtype)


def flash_attention(qt, k, vt, *, group, q_tiles, tks, q_row0, n_q, k_row0, n_k):
    hk, _, dq = k.shape
    dv = vt.shape[1]
    tq = ROW_TILE * q_tiles
    assert q_row0 % tq == 0 and n_q % tq == 0 and k_row0 % n_k == 0 and n_k % tks == 0
    assert tks % KEY_CHUNK == 0
    qb, kb = q_row0 // tq, k_row0 // n_k
    n_str = group * q_tiles
    return pl.pallas_call(
        functools.partial(_flash_kernel, group, q_tiles, tks),
        grid=(hk, n_q // tq),
        in_specs=[pl.BlockSpec((group, dq, tq), lambda h, i: (h, 0, qb + i)),
                  pl.BlockSpec((1, n_k, dq), lambda h, i: (h, kb, 0)),
                  pl.BlockSpec((1, dv, n_k), lambda h, i: (h, 0, kb))],
        out_specs=pl.BlockSpec((tq, group * dv), lambda h, i: (i, h)),
        out_shape=jax.ShapeDtypeStruct((n_q, hk * group * dv), jnp.bfloat16),
        scratch_shapes=[pltpu.VMEM((n_str, tks, ROW_TILE), jnp.float32),
                        pltpu.VMEM((n_str, tks, ROW_TILE), jnp.float32),
                        pltpu.VMEM((n_str, 1, ROW_TILE), jnp.float32),
                        pltpu.VMEM((n_str, 1, ROW_TILE), jnp.float32),
                        pltpu.VMEM((n_str, dv, ROW_TILE), jnp.float32)],
        compiler_params=_cparams(("arbitrary", "arbitrary")),
        name="flash_attention",
    )(qt, k, vt)


def _gla_block(q_ref, k_ref, v_ref, c_ref, o_ref, s_ref, ge_ref, bd_ref, r0, forward):
    nb = GLA_BLOCK
    rows = pl.ds(r0, nb)
    q, k, v, b = q_ref[rows, :], k_ref[rows, :], v_ref[rows, :], c_ref[rows, :]
    total = b[nb - 1:nb, :] if forward else b[0:1, :]
    st = s_ref[...]
    qe = (q * jnp.exp(b)).astype(jnp.bfloat16)
    o = _dot_nt(qe, st.astype(jnp.bfloat16))
    rid = lax.broadcasted_iota(jnp.int32, b.shape, 0)
    terms = []
    for j in range(nb):
        keep = (rid >= j) if forward else (rid <= j)
        w = jnp.exp(jnp.where(keep, b - b[j:j + 1, :], -jnp.inf))
        terms.append((q * w * k[j:j + 1, :]).astype(jnp.bfloat16))
    scores = _dot(jnp.concatenate(terms, axis=0), ge_ref[...])
    for j in range(nb):
        o = o + scores[j * nb:(j + 1) * nb, :] * v[j:j + 1, :]
    o_ref[rows, :] = o
    ke = (k * jnp.exp(total - b)).astype(jnp.bfloat16)
    upd = lax.dot_general(v.astype(jnp.bfloat16), ke, (((0,), (0,)), ((), ())),
                          preferred_element_type=jnp.float32)
    s_ref[...] = jnp.exp(total) * st + bd_ref[...] * upd


def _gla_kernel(qf, kf, vf, mf, qb, kb, vb, mb, wgf, bgf, wgb, bgb, tri_f, tri_b, ge, bd,
                of, ob, cf_ref, cb_ref, sf_ref, sb_ref):
    @pl.when(pl.program_id(0) == 0)
    def _():
        sf_ref[...] = jnp.zeros(sf_ref.shape, jnp.float32)
        sb_ref[...] = jnp.zeros(sb_ref.shape, jnp.float32)

    def cum_log_decay(m_ref, w_ref, b_ref, tri_ref):
        z = jnp.dot(m_ref[...], w_ref[...], precision=lax.Precision.HIGHEST,
                    preferred_element_type=jnp.float32) + b_ref[...]
        la = (jnp.minimum(z, 0.0) - jnp.log1p(jnp.exp(-jnp.abs(z)))) * (1.0 / GLA_GATE_NORM)
        return jnp.dot(tri_ref[...], la, precision=lax.Precision.HIGHEST,
                       preferred_element_type=jnp.float32)

    cf_ref[...] = cum_log_decay(mf, wgf, bgf, tri_f)
    cb_ref[...] = cum_log_decay(mb, wgb, bgb, tri_b)
    n_blocks = qf.shape[0] // GLA_BLOCK

    def body(t, carry):
        r_f = pl.multiple_of(t * GLA_BLOCK, GLA_BLOCK)
        r_b = pl.multiple_of((n_blocks - 1 - t) * GLA_BLOCK, GLA_BLOCK)
        _gla_block(qf, kf, vf, cf_ref, of, sf_ref, ge, bd, r_f, True)
        _gla_block(qb, kb, vb, cb_ref, ob, sb_ref, ge, bd, r_b, False)
        return carry

    lax.fori_loop(0, n_blocks, body, 0)


def gla_scan(p, w_gf_p, b_gf, w_gb_p, b_gb, consts):
    N = p.shape[0]
    tm = ROW_TILE
    nt = N // tm
    nk, nv = GLA_HEADS * GLA_DK, GLA_HEADS * GLA_DV
    fwd = lambda i: (i + nt - 1) % nt
    bwd = lambda i: (2 * nt - 1 - i) % nt
    col = lambda w, c, order: pl.BlockSpec((tm, w), lambda i: (order(i), c))
    full = lambda a: pl.BlockSpec(a.shape, lambda i: (0,) * a.ndim)
    side = lambda order: [col(nk, COL_LQ // nk, order), col(nk, COL_LK // nk, order),
                          col(nv, COL_LV // nv, order), col(LANES, COL_MISC // LANES, order)]
    small = (w_gf_p, b_gf, w_gb_p, b_gb) + tuple(consts)
    return pl.pallas_call(
        _gla_kernel,
        grid=(nt,),
        in_specs=side(fwd) + side(bwd) + [full(a) for a in small],
        out_specs=[pl.BlockSpec((tm, nv), lambda i: (fwd(i), 0)),
                   pl.BlockSpec((tm, nv), lambda i: (bwd(i), 0))],
        out_shape=[jax.ShapeDtypeStruct((N, nv), jnp.float32)] * 2,
        scratch_shapes=[pltpu.VMEM((tm, nk), jnp.float32), pltpu.VMEM((tm, nk), jnp.float32),
                        pltpu.VMEM((nv, nk), jnp.float32), pltpu.VMEM((nv, nk), jnp.float32)],
        compiler_params=_cparams(("arbitrary",)),
        name="gla_scan",
    )(p, p, p, p, p, p, p, p, *small)


def _gla_consts():
    i = jnp.arange(ROW_TILE)
    same = (i[:, None] // GLA_BLOCK) == (i[None, :] // GLA_BLOCK)
    tri_f = (same & (i[None, :] <= i[:, None])).astype(jnp.float32)
    tri_b = (same & (i[None, :] >= i[:, None])).astype(jnp.float32)
    hk = jnp.arange(GLA_HEADS * GLA_DK) // GLA_DK
    hv = jnp.arange(GLA_HEADS * GLA_DV) // GLA_DV
    ge = (hk[:, None] == hv[None, :]).astype(jnp.bfloat16)
    bd = (hv[:, None] == hk[None, :]).astype(jnp.float32)
    return tri_f, tri_b, ge, bd


def _prep_gate(w_f, b_f, w_b, b_b):
    z = jnp.zeros((LANES, w_f.shape[1]), jnp.float32)
    return (z.at[MLA_ROPE:MLA_ROPE + GLA_GATE_RANK].set(w_f), b_f[None],
            z.at[MLA_ROPE + GLA_GATE_RANK:MLA_ROPE + 2 * GLA_GATE_RANK].set(w_b), b_b[None])


def _layer_norm(u, g, b):
    mu = jnp.mean(u, axis=-1, keepdims=True)
    d = u - mu
    var = jnp.mean(d * d, axis=-1, keepdims=True)
    return d * lax.rsqrt(var + LN_EPS) * g + b


def _first_max(rows):
    best, idx = rows[0], jnp.zeros(rows[0].shape, jnp.int32)
    for i in range(1, len(rows)):
        better = rows[i] > best
        idx = jnp.where(better, i, idx)
        best = jnp.where(better, rows[i], best)
    return best, idx


def _route(lt):
    grp_rows = [lt[g:g + 1, :] for g in range(N_GROUPS)]
    g_best, grp = _first_max(grp_rows)
    grp_w = 1.0 / sum(jnp.exp(r - g_best) for r in grp_rows)
    in_grp = []
    for j in range(EXPERTS_PER_GROUP):
        val = lt[8 + j:9 + j, :]
        for g in range(1, N_GROUPS):
            r = 8 + g * EXPERTS_PER_GROUP + j
            val = jnp.where(grp == g, lt[r:r + 1, :], val)
        in_grp.append(val)
    v1, i1 = _first_max(in_grp)
    rest = [jnp.where(i1 == j, -jnp.inf, in_grp[j]) for j in range(EXPERTS_PER_GROUP)]
    v2, i2 = _first_max(rest)
    e = jnp.exp(v2 - v1)
    w1 = 1.0 / (1.0 + e)
    ids = jnp.concatenate([grp * EXPERTS_PER_GROUP + i1, grp * EXPERTS_PER_GROUP + i2], axis=0)
    gates = jnp.concatenate([grp_w * w1, grp_w * (e * w1)], axis=0)
    return ids, gates


def _outproj_kernel(n_lat_tiles, alpha,
                    x_ref, yml, ymc, ygl, ygc, of_ref, ob_ref, g_ref, on_ref, wo_ref,
                    g1_ref, sh2_ref, sc2_ref, lg_ref, lb_ref, wr_ref, br_ref,
                    x1_ref, h2_ref, id_ref, gt_ref):
    is_ctx = pl.program_id(0) >= n_lat_tiles
    y_mla = jnp.where(is_ctx, ymc[...], yml[...])
    y_gqa = jnp.where(is_ctx, ygc[...], ygl[...])
    o = of_ref[...] + ob_ref[...]
    heads = [_rms(o[:, h * GLA_DV:(h + 1) * GLA_DV], on_ref[...]) for h in range(GLA_HEADS)]
    g = g_ref[...]
    y_gla = (jnp.concatenate(heads, axis=1) * (g * jax.nn.sigmoid(g))).astype(jnp.bfloat16)
    w_mla, w_gqa = y_mla.shape[1], y_gqa.shape[1]
    y = (_dot(y_mla, wo_ref[0:w_mla, :]) + _dot(y_gqa, wo_ref[w_mla:w_mla + w_gqa, :])
         + _dot(y_gla, wo_ref[w_mla + w_gqa:, :]))
    x1 = _layer_norm(alpha * x_ref[...] + _mod_row(g1_ref, is_ctx) * y, lg_ref[...], lb_ref[...])
    x1_ref[...] = x1
    h2 = x1 * (1.0 + _mod_row(sc2_ref, is_ctx)) + _mod_row(sh2_ref, is_ctx)
    h2_ref[...] = h2
    lt = lax.dot_general(wr_ref[...], h2, (((1,), (1,)), ((), ())), precision=lax.Precision.HIGHEST,
                         preferred_element_type=jnp.float32) + br_ref[...]
    ids, gates = _route(lt)
    rid = lax.broadcasted_iota(jnp.int32, id_ref.shape, 0)
    id_ref[...] = jnp.where(rid == 0, ids[0:1], jnp.where(rid == 1, ids[1:2], 0))
    gt_ref[...] = jnp.where(rid == 0, gates[0:1], jnp.where(rid == 1, gates[1:2], 0.0))


def out_projection(xs, y_mla, y_gqa, o_f, o_b, p, out_norm, w_out, mod_l, ln_g, ln_b, w_rt, b_rt,
                   n_lat, alpha):
    N, D = xs.shape
    tm = ROW_TILE
    nl = n_lat // tm
    assert N - n_lat == tm
    nv = GLA_HEADS * GLA_DV
    lat = lambda a: pl.BlockSpec((tm, a.shape[1]), lambda i: (jnp.minimum(i, nl - 1), 0))
    ctx = lambda a: pl.BlockSpec((tm, a.shape[1]), lambda i: (0, 0))
    row = lambda w, c: pl.BlockSpec((tm, w), lambda i: (i, c))
    full = lambda a: pl.BlockSpec(a.shape, lambda i: (0,) * a.ndim)
    modc = lambda c: pl.BlockSpec((8, D), lambda i: (0, c))
    (yml, ymc), (ygl, ygc) = y_mla, y_gqa
    return pl.pallas_call(
        functools.partial(_outproj_kernel, nl, alpha),
        grid=(N // tm,),
        in_specs=[row(D, 0), lat(yml), ctx(ymc), lat(ygl), ctx(ygc), row(nv, 0), row(nv, 0),
                  row(nv, COL_LG // nv), full(out_norm), full(w_out),
                  modc(2), modc(3), modc(4), full(ln_g), full(ln_b), full(w_rt), full(b_rt)],
        out_specs=[row(D, 0), row(D, 0), pl.BlockSpec((8, tm), lambda i: (0, i)),
                   pl.BlockSpec((8, tm), lambda i: (0, i))],
        out_shape=[jax.ShapeDtypeStruct((N, D), jnp.float32), jax.ShapeDtypeStruct((N, D), jnp.float32),
                   jax.ShapeDtypeStruct((8, N), jnp.int32), jax.ShapeDtypeStruct((8, N), jnp.float32)],
        compiler_params=_cparams(("arbitrary",)),
        name="out_projection",
    )(xs, yml, ymc, ygl, ygc, o_f, o_b, p, out_norm, w_out, mod_l, mod_l, mod_l, ln_g, ln_b, w_rt, b_rt)


def _row_copy(src_hbm, idx_ref, dst, sem, r):
    return pltpu.make_async_copy(src_hbm.at[pl.ds(idx_ref[0, 0, r], 1), :], dst.at[pl.ds(r, 1), :], sem)


def _gather_rows(src_hbm, idx_ref, dst, sem, n):
    def start(r, c):
        _row_copy(src_hbm, idx_ref, dst, sem, r).start()
        return c

    def wait(r, c):
        _row_copy(src_hbm, idx_ref, dst, sem, r).wait()
        return c

    lax.fori_loop(0, n, start, 0)
    lax.fori_loop(0, n, wait, 0)


def _expert_kernel(te_ref, nu_ref, idx_ref, h_hbm, wg_ref, wu_ref, wd_ref, y_ref, xbuf, sem):
    i = pl.program_id(0)

    @pl.when(i < nu_ref[0])
    def _():
        _gather_rows(h_hbm, idx_ref, xbuf, sem, xbuf.shape[0])
        x = xbuf[...].astype(jnp.bfloat16)
        a = _dot(x, wg_ref[0])
        u = _dot(x, wu_ref[0])
        hid = (a * jax.nn.sigmoid(a) * u).astype(jnp.bfloat16)
        y_ref[...] = _dot(hid, wd_ref[0])

    @pl.when(i >= nu_ref[0])
    def _():
        y_ref[...] = jnp.zeros(y_ref.shape, jnp.float32)


def expert_ffn(h2, tile_expert, n_used, sorted_tok, w_g, w_u, w_d):
    N, D = h2.shape
    tm = MOE_TILE
    n_tiles = sorted_tok.shape[0] // tm
    E, _, F = w_g.shape
    gs = pltpu.PrefetchScalarGridSpec(
        num_scalar_prefetch=2,
        grid=(n_tiles,),
        in_specs=[pl.BlockSpec((1, 1, tm), lambda i, te, nu: (i, 0, 0), memory_space=pltpu.SMEM),
                  pl.BlockSpec(memory_space=pl.ANY),
                  pl.BlockSpec((1, D, F), lambda i, te, nu: (te[i], 0, 0)),
                  pl.BlockSpec((1, D, F), lambda i, te, nu: (te[i], 0, 0)),
                  pl.BlockSpec((1, F, D), lambda i, te, nu: (te[i], 0, 0))],
        out_specs=pl.BlockSpec((tm, D), lambda i, te, nu: (i, 0)),
        scratch_shapes=[pltpu.VMEM((tm, D), jnp.float32), pltpu.SemaphoreType.DMA(())],
    )
    return pl.pallas_call(
        _expert_kernel,
        grid_spec=gs,
        out_shape=jax.ShapeDtypeStruct((n_tiles * tm, D), jnp.float32),
        compiler_params=_cparams(("arbitrary",)),
        name="expert_ffn",
    )(tile_expert, n_used, sorted_tok.reshape(n_tiles, 1, tm), h2, w_g, w_u, w_d)


def _combine_kernel(n_lat_tiles, alpha, pos_ref, y_hbm, x_ref, gc_ref, g2_ref, lg_ref, lb_ref,
                    o_ref, ybuf, sem):
    is_ctx = pl.program_id(0) >= n_lat_tiles
    tm = x_ref.shape[0]
    _gather_rows(y_hbm, pos_ref, ybuf, sem, 2 * tm)
    gc = gc_ref[...]
    f = gc[:, 0:1] * ybuf[0:tm, :] + gc[:, 1:2] * ybuf[tm:2 * tm, :]
    o_ref[...] = _layer_norm(alpha * x_ref[...] + _mod_row(g2_ref, is_ctx) * f, lg_ref[...], lb_ref[...])


def moe_combine(xs1, y_sorted, pos, gate_cols, mod_l, ln_g, ln_b, n_lat, alpha):
    N, D = xs1.shape
    tm = ROW_TILE
    full = lambda a: pl.BlockSpec(a.shape, lambda i: (0,) * a.ndim)
    return pl.pallas_call(
        functools.partial(_combine_kernel, n_lat // tm, alpha),
        grid=(N // tm,),
        in_specs=[pl.BlockSpec((1, 1, 2 * tm), lambda i: (i, 0, 0), memory_space=pltpu.SMEM),
                  pl.BlockSpec(memory_space=pl.ANY),
                  pl.BlockSpec((tm, D), lambda i: (i, 0)),
                  pl.BlockSpec((tm, 2), lambda i: (i, 0)),
                  pl.BlockSpec((8, D), lambda i: (0, 5)), full(ln_g), full(ln_b)],
        out_specs=pl.BlockSpec((tm, D), lambda i: (i, 0)),
        out_shape=jax.ShapeDtypeStruct((N, D), jnp.float32),
        scratch_shapes=[pltpu.VMEM((2 * tm, D), jnp.float32), pltpu.SemaphoreType.DMA(())],
        compiler_params=_cparams(("arbitrary",)),
        name="moe_combine",
    )(pos, y_sorted, xs1, gate_cols, mod_l, ln_g, ln_b)


def _dispatch_plan(ids, n_tok):
    tm = MOE_TILE
    n_asg = 2 * n_tok
    n_tiles = n_asg // tm + N_EXPERTS
    e_flat = ids.reshape(n_asg)
    tok_flat = jnp.tile(jnp.arange(n_tok, dtype=jnp.int32), 2)
    order = jnp.argsort(e_flat, stable=True).astype(jnp.int32)
    sorted_e = e_flat[order]
    experts = jnp.arange(N_EXPERTS, dtype=jnp.int32)
    starts = jnp.searchsorted(sorted_e, experts, side="left").astype(jnp.int32)
    ends = jnp.searchsorted(sorted_e, experts, side="right").astype(jnp.int32)
    padded = ((ends - starts + tm - 1) // tm) * tm
    pad_ends = jnp.cumsum(padded).astype(jnp.int32)
    pad_starts = pad_ends - padded
    dest = pad_starts[sorted_e] + (jnp.arange(n_asg, dtype=jnp.int32) - starts[sorted_e])
    sorted_tok = jnp.zeros((n_tiles * tm,), jnp.int32).at[dest].set(tok_flat[order])
    pos = jnp.zeros((n_asg,), jnp.int32).at[order].set(dest)
    n_used = (pad_ends[-1] // tm).reshape(1)
    tile_start = jnp.arange(n_tiles, dtype=jnp.int32) * tm
    tile_expert = jnp.searchsorted(pad_ends, tile_start, side="right").astype(jnp.int32)
    last = jnp.searchsorted(pad_ends, pad_ends[-1] - 1, side="right").astype(jnp.int32)
    tile_expert = jnp.minimum(tile_expert, last)
    nt = n_tok // ROW_TILE
    pos_tiles = jnp.concatenate([pos[:n_tok].reshape(nt, 1, ROW_TILE),
                                 pos[n_tok:].reshape(nt, 1, ROW_TILE)], axis=2)
    return sorted_tok, tile_expert, n_used, pos_tiles


def _rope_tables(n_lat, n_ctx, dim):
    rows = n_lat // GRID_W
    row = jnp.repeat(jnp.arange(rows), GRID_W).astype(jnp.float32)
    col = jnp.tile(jnp.arange(GRID_W), rows).astype(jnp.float32)
    half = dim // 2
    inv_freq = ROPE_THETA ** (-jnp.arange(0, half, 2, dtype=jnp.float32) / half)
    ang_r = row[:, None] * inv_freq
    ang_c = col[:, None] * inv_freq
    ang = jnp.concatenate([ang_r, ang_r, ang_c, ang_c], axis=-1)
    cos, sin = jnp.cos(ang), jnp.sin(ang)
    q = dim // 4
    lane = jnp.arange(dim)
    first = ((lane // q) % 2) == 0
    sin_a = jnp.where(first, -sin, 0.0)
    sin_b = jnp.where(first, 0.0, sin)

    def pad(t, fill):
        t = jnp.pad(t, ((0, 0), (0, LANES - dim)), constant_values=fill)
        return jnp.pad(t, ((0, n_ctx), (0, 0)), constant_values=fill)

    return pad(cos, 1.0), pad(sin_a, 0.0), pad(sin_b, 0.0)


def _in_perm():
    a = MLA_Q_RANK + MLA_KV_RANK + MLA_ROPE
    b = a + (GQA_HEADS + 2 * GQA_KV_HEADS) * HEAD_DIM
    nk, nv = GLA_HEADS * GLA_DK, GLA_HEADS * GLA_DV
    r = lambda s, n: list(range(s, s + n))
    cols = (r(0, MLA_Q_RANK + MLA_KV_RANK)
            + r(a, b - a)
            + r(b, 2 * nk + 2 * nv)
            + r(MLA_Q_RANK + MLA_KV_RANK, MLA_ROPE)
            + r(b + 2 * nk + 2 * nv, 2 * GLA_GATE_RANK))
    return jnp.asarray(cols, jnp.int32)


def _prep_w_in(w_in_l):
    w = w_in_l[:, _in_perm()]
    scale = jnp.ones((w.shape[1],), jnp.float32).at[COL_LQ:COL_LK].set(GLA_DK ** -0.5)
    w = w * scale
    return jnp.pad(w, ((0, 0), (0, P_COLS - w.shape[1]))).astype(jnp.bfloat16)


def _prep_w_uq(w_uq_l):
    r = w_uq_l.shape[0]
    w = w_uq_l.reshape(r, MLA_HEADS, MLA_NOPE + MLA_ROPE)
    w = jnp.pad(w, ((0, 0), (0, 0), (0, MLA_QPAD - MLA_NOPE - MLA_ROPE)))
    return w.reshape(r, MLA_HEADS * MLA_QPAD).astype(jnp.bfloat16)


def _prep_router(w_rg, b_rg, w_re, b_re):
    D = w_rg.shape[0]
    w = jnp.zeros((LANES, D), jnp.float32).at[0:N_GROUPS].set(w_rg.T).at[8:8 + N_EXPERTS].set(w_re.T)
    b = jnp.zeros((LANES,), jnp.float32).at[0:N_GROUPS].set(b_rg).at[8:8 + N_EXPERTS].set(b_re)
    return w, b[:, None]


def _kv_tile(n):
    return max(t for t in (256, 640, 1280) if n % t == 0)


def kernel(x, c, ctx, c_ctx, ada_w, ada_b, w_in, mla_q_norm, mla_w_uq, mla_kv_norm, mla_w_ukv, gqa_q_norm, gqa_k_norm, gla_w_gate_fwd, gla_b_gate_fwd, gla_w_gate_bwd, gla_b_gate_bwd, gla_out_norm, w_out, ln1_g, ln1_b, w_route_group, b_route_group, w_route_expert, b_route_expert, w_expert_gate, w_expert_up, w_expert_down, ln2_g, ln2_b):
    T, C = x.shape[1], ctx.shape[1]
    N = T + C
    L = ada_w.shape[0]
    assert x.shape[0] == 1 and C == ROW_TILE and T % ROW_TILE == 0 and T % GRID_W == 0
    alpha = (2.0 * L) ** 0.25
    bf = jnp.bfloat16

    xs = jnp.concatenate([x[0], ctx[0]], axis=0)
    mods = modulation(c, c_ctx, ada_w, ada_b)
    tabs_mla = _rope_tables(T, C, MLA_ROPE)
    tabs_gqa = _rope_tables(T, C, HEAD_DIM)
    gla_consts = _gla_consts()
    mla_scale = (MLA_NOPE + MLA_ROPE) ** -0.5 * LOG2E
    gqa_scale = HEAD_DIM ** -0.5 * LOG2E
    tabs_mla_t = tuple(t.T for t in tabs_mla)
    tabs_gqa_t = tuple(t.T for t in tabs_gqa)
    lat = dict(q_row0=0, n_q=T, k_row0=0, n_k=N, tks=_kv_tile(N))
    con = dict(q_row0=T, n_q=C, k_row0=T, n_k=C, tks=C)

    for l in range(L):
        mod_l = mods[l]
        p = in_projection(xs, mod_l, _prep_w_in(w_in[l]), T)

        w_ukv = mla_w_ukv[l].reshape(MLA_KV_RANK, MLA_HEADS, MLA_NOPE + MLA_V)
        w_kn = w_ukv[:, :, :MLA_NOPE].reshape(MLA_KV_RANK, MLA_HEADS * MLA_NOPE).astype(bf)
        w_v_t = w_ukv[:, :, MLA_NOPE:].reshape(MLA_KV_RANK, MLA_HEADS * MLA_V).T.astype(bf)
        qt, k, vt = mla_prep(p, (mla_q_norm[l] * mla_scale)[None], _prep_w_uq(mla_w_uq[l]).T,
                             mla_kv_norm[l][None], w_kn, w_v_t, tabs_mla, tabs_mla_t)
        y_mla = (flash_attention(qt, k, vt, group=1, q_tiles=2 if T % (2 * ROW_TILE) == 0 else 1, **lat),
                 flash_attention(qt, k, vt, group=1, q_tiles=1, **con))
        qt, k, vt = gqa_prep(p, (gqa_q_norm[l] * gqa_scale)[:, None], gqa_k_norm[l][None],
                             tabs_gqa, tabs_gqa_t)
        y_gqa = (flash_attention(qt, k, vt, group=GQA_GROUP, q_tiles=1, **lat),
                 flash_attention(qt, k, vt, group=GQA_GROUP, q_tiles=1, **con))
        o_f, o_b = gla_scan(p, *_prep_gate(gla_w_gate_fwd[l], gla_b_gate_fwd[l],
                                           gla_w_gate_bwd[l], gla_b_gate_bwd[l]), gla_consts)

        w_rt, b_rt = _prep_router(w_route_group[l], b_route_group[l], w_route_expert[l], b_route_expert[l])
        x1, h2, ids, gates = out_projection(xs, y_mla, y_gqa, o_f, o_b, p, gla_out_norm[l][None],
                                            w_out[l].astype(bf), mod_l, ln1_g[l][None], ln1_b[l][None],
                                            w_rt, b_rt, T, alpha)

        sorted_tok, tile_expert, n_used, pos_tiles = _dispatch_plan(ids[0:2], N)
        y_sorted = expert_ffn(h2, tile_expert, n_used, sorted_tok, w_expert_gate[l].astype(bf),
                              w_expert_up[l].astype(bf), w_expert_down[l].astype(bf))
        xs = moe_combine(x1, y_sorted, pos_tiles, gates[0:2].T, mod_l, ln2_g[l][None], ln2_b[l][None],
                         T, alpha)
    return xs[:T][None]
```

```python
import functools
import math

import jax
import jax.numpy as jnp
from jax import lax
from jax.experimental import pallas as pl
from jax.experimental.pallas import tpu as pltpu

GRID_W = 64
ROPE_THETA = 10000.0
NORM_EPS = 1e-6
LN_EPS = 1e-5
HEAD_DIM = 128
MLA_HEADS = 6
MLA_Q_RANK = 512
MLA_KV_RANK = 256
MLA_NOPE = 128
MLA_ROPE = 64
MLA_V = 128
GQA_HEADS = 6
GQA_KV_HEADS = 2
GQA_GROUP = GQA_HEADS // GQA_KV_HEADS
GLA_HEADS = 4
GLA_DK = 64
GLA_DV = 128
GLA_GATE_RANK = 16
GLA_GATE_NORM = 16.0
N_GROUPS = 4
EXPERTS_PER_GROUP = 8
N_EXPERTS = N_GROUPS * EXPERTS_PER_GROUP
N_MOD = 6

LANES = 128
VMEM_LIMIT = 48 * 1024 * 1024

ROW_TILE = 256
GLA_BLOCK = 16
MOE_TILE = 256

COL_CQ = 0
COL_CKV = 512
COL_GQ = 768
COL_GK = 1536
COL_GV = 1792
COL_LQ = 2048
COL_LK = 2304
COL_LV = 2560
COL_LG = 3072
COL_MISC = 3584
P_COLS = 3840
P_HALF = P_COLS // 2
MLA_QPAD = 256

LOG2E = math.log2(math.e)


def _cparams(sem):
    return pltpu.CompilerParams(dimension_semantics=sem, vmem_limit_bytes=VMEM_LIMIT)


def _dot(a, b):
    return jnp.dot(a, b, preferred_element_type=jnp.float32)


def _dot_nt(a, b):
    return lax.dot_general(a, b, (((1,), (1,)), ((), ())), preferred_element_type=jnp.float32)


def _mod_kernel(crep_ref, w_ref, b_ref, o_ref):
    tn = w_ref.shape[2]
    rows = []
    for r in range(2):
        cr = crep_ref[r]
        a = cr * jax.nn.sigmoid(cr)
        parts = []
        for j in range(tn // LANES):
            wj = w_ref[0, :, j * LANES:(j + 1) * LANES]
            parts.append(jnp.sum(wj * a, axis=0, keepdims=True))
        rows.append(jnp.concatenate(parts, axis=1) + b_ref[0])
    rid = lax.broadcasted_iota(jnp.int32, (8, tn), 0)
    o_ref[0] = jnp.where(rid == 0, rows[0], jnp.where(rid == 1, rows[1], 0.0))


def modulation(c, c_ctx, ada_w, ada_b):
    L, D, M = ada_w.shape
    tn = max(t for t in range(LANES, 1024 + 1, LANES) if M % t == 0)
    crep = jnp.broadcast_to(jnp.stack([c[0], c_ctx])[:, :, None], (2, D, LANES))
    return pl.pallas_call(
        _mod_kernel,
        grid=(L, M // tn),
        in_specs=[pl.BlockSpec((2, D, LANES), lambda l, j: (0, 0, 0)),
                  pl.BlockSpec((1, D, tn), lambda l, j: (l, 0, j)),
                  pl.BlockSpec((1, 1, tn), lambda l, j: (l, 0, j))],
        out_specs=pl.BlockSpec((1, 8, tn), lambda l, j: (l, 0, j)),
        out_shape=jax.ShapeDtypeStruct((L, 8, M), jnp.float32),
        compiler_params=_cparams(("arbitrary", "arbitrary")),
        name="modulation",
    )(crep, ada_w, ada_b.reshape(L, 1, M))


def _mod_row(ref, is_ctx):
    return jnp.where(is_ctx, ref[1:2, :], ref[0:1, :])


def _inproj_kernel(n_lat_tiles, x_ref, sh_ref, sc_ref, w_ref, o_ref):
    is_ctx = pl.program_id(1) >= n_lat_tiles
    h = x_ref[...] * (1.0 + _mod_row(sc_ref, is_ctx)) + _mod_row(sh_ref, is_ctx)
    o_ref[...] = _dot(h.astype(jnp.bfloat16), w_ref[...])


def in_projection(xs, mod_l, w_in_p, n_lat):
    N, D = xs.shape
    tm = ROW_TILE
    return pl.pallas_call(
        functools.partial(_inproj_kernel, n_lat // tm),
        grid=(2, N // tm),
        in_specs=[pl.BlockSpec((tm, D), lambda j, i: (i, 0)),
                  pl.BlockSpec((8, D), lambda j, i: (0, 0)),
                  pl.BlockSpec((8, D), lambda j, i: (0, 1)),
                  pl.BlockSpec((D, P_HALF), lambda j, i: (0, j))],
        out_specs=pl.BlockSpec((tm, P_HALF), lambda j, i: (i, j)),
        out_shape=jax.ShapeDtypeStruct((N, P_COLS), jnp.float32),
        compiler_params=_cparams(("arbitrary", "arbitrary")),
        name="in_projection",
    )(xs, mod_l, mod_l, w_in_p)


def _rope(x, cos, sin_a, sin_b, quarter):
    w = x.shape[-1]
    return x * cos + pltpu.roll(x, w - quarter, 1) * sin_a + pltpu.roll(x, quarter, 1) * sin_b


def _rms(x, gain):
    return x * lax.rsqrt(jnp.mean(x * x, axis=-1, keepdims=True) + NORM_EPS) * gain


def _rope_t(x, cos, sin_a, sin_b, quarter):
    w = x.shape[0]
    return x * cos + pltpu.roll(x, w - quarter, 0) * sin_a + pltpu.roll(x, quarter, 0) * sin_b


def _mla_prep_kernel(cq_ref, ckv_ref, misc_ref, qn_ref, wuqt_ref, kvn_ref, wkn_ref, wvt_ref,
                     cos_ref, sa_ref, sb_ref, cost_ref, sat_ref, sbt_ref, qt_ref, k_ref, vt_ref):
    cq = _rms(cq_ref[...], qn_ref[...]).astype(jnp.bfloat16)
    qt_all = _dot_nt(wuqt_ref[...], cq)
    ckv = _rms(ckv_ref[...], kvn_ref[...]).astype(jnp.bfloat16)
    kn_all = _dot(ckv, wkn_ref[...])
    vt_all = _dot_nt(wvt_ref[...], ckv)
    lane = lax.broadcasted_iota(jnp.int32, misc_ref.shape, 1)
    k_rope = _rope(misc_ref[...], cos_ref[...], sa_ref[...], sb_ref[...], MLA_ROPE // 4)
    k_rope = jnp.where(lane < MLA_ROPE, k_rope, 0.0).astype(jnp.bfloat16)
    cost, sat, sbt = cost_ref[...], sat_ref[...], sbt_ref[...]
    for h in range(MLA_HEADS):
        b0 = h * MLA_QPAD
        qt_ref[h, 0:LANES, :] = qt_all[b0:b0 + LANES, :].astype(jnp.bfloat16)
        qt_ref[h, LANES:2 * LANES, :] = _rope_t(qt_all[b0 + LANES:b0 + 2 * LANES, :], cost, sat, sbt,
                                                MLA_ROPE // 4).astype(jnp.bfloat16)
        k_ref[h, :, 0:LANES] = kn_all[:, h * MLA_NOPE:(h + 1) * MLA_NOPE].astype(jnp.bfloat16)
        k_ref[h, :, LANES:2 * LANES] = k_rope
        vt_ref[h] = vt_all[h * MLA_V:(h + 1) * MLA_V, :].astype(jnp.bfloat16)


def mla_prep(p, q_gain, w_uq_t, kv_gain, w_kn, w_v_t, tabs, tabs_t):
    N = p.shape[0]
    tm = ROW_TILE
    H = MLA_HEADS
    row = lambda w, c: pl.BlockSpec((tm, w), lambda i: (i, c))
    full = lambda a: pl.BlockSpec(a.shape, lambda i: (0,) * a.ndim)
    tab = pl.BlockSpec((tm, LANES), lambda i: (i, 0))
    tab_t = pl.BlockSpec((LANES, tm), lambda i: (0, i))
    return pl.pallas_call(
        _mla_prep_kernel,
        grid=(N // tm,),
        in_specs=[row(MLA_Q_RANK, COL_CQ // MLA_Q_RANK), row(MLA_KV_RANK, COL_CKV // MLA_KV_RANK),
                  row(LANES, COL_MISC // LANES),
                  full(q_gain), full(w_uq_t), full(kv_gain), full(w_kn), full(w_v_t),
                  tab, tab, tab, tab_t, tab_t, tab_t],
        out_specs=[pl.BlockSpec((H, MLA_QPAD, tm), lambda i: (0, 0, i)),
                   pl.BlockSpec((H, tm, MLA_QPAD), lambda i: (0, i, 0)),
                   pl.BlockSpec((H, MLA_V, tm), lambda i: (0, 0, i))],
        out_shape=[jax.ShapeDtypeStruct((H, MLA_QPAD, N), jnp.bfloat16),
                   jax.ShapeDtypeStruct((H, N, MLA_QPAD), jnp.bfloat16),
                   jax.ShapeDtypeStruct((H, MLA_V, N), jnp.bfloat16)],
        compiler_params=_cparams(("arbitrary",)),
        name="mla_prep",
    )(p, p, p, q_gain, w_uq_t, kv_gain, w_kn, w_v_t, *tabs, *tabs_t)


def _gqa_prep_kernel(q_in, k_in, v_in, qn_ref, kn_ref, cos_ref, sa_ref, sb_ref,
                     cost_ref, sat_ref, sbt_ref, qt_ref, k_ref, vt_ref):
    cos, sa, sb = cos_ref[...], sa_ref[...], sb_ref[...]
    cost, sat, sbt = cost_ref[...], sat_ref[...], sbt_ref[...]
    qt = HEAD_DIM // 4
    for h in range(GQA_HEADS):
        xt = q_in[:, h * HEAD_DIM:(h + 1) * HEAD_DIM].T
        xt = xt * lax.rsqrt(jnp.mean(xt * xt, axis=0, keepdims=True) + NORM_EPS) * qn_ref[...]
        qt_ref[h] = _rope_t(xt, cost, sat, sbt, qt).astype(jnp.bfloat16)
    for h in range(GQA_KV_HEADS):
        x = _rms(k_in[:, h * HEAD_DIM:(h + 1) * HEAD_DIM], kn_ref[...])
        k_ref[h] = _rope(x, cos, sa, sb, qt).astype(jnp.bfloat16)
        vt_ref[h] = v_in[:, h * HEAD_DIM:(h + 1) * HEAD_DIM].T.astype(jnp.bfloat16)


def gqa_prep(p, q_gain_col, k_gain, tabs, tabs_t):
    N = p.shape[0]
    tm = ROW_TILE
    wq, wk = GQA_HEADS * HEAD_DIM, GQA_KV_HEADS * HEAD_DIM
    row = lambda w, c: pl.BlockSpec((tm, w), lambda i: (i, c))
    full = lambda a: pl.BlockSpec(a.shape, lambda i: (0,) * a.ndim)
    tab = pl.BlockSpec((tm, LANES), lambda i: (i, 0))
    tab_t = pl.BlockSpec((LANES, tm), lambda i: (0, i))
    return pl.pallas_call(
        _gqa_prep_kernel,
        grid=(N // tm,),
        in_specs=[row(wq, COL_GQ // wq), row(wk, COL_GK // wk), row(wk, COL_GV // wk),
                  full(q_gain_col), full(k_gain), tab, tab, tab, tab_t, tab_t, tab_t],
        out_specs=[pl.BlockSpec((GQA_HEADS, HEAD_DIM, tm), lambda i: (0, 0, i)),
                   pl.BlockSpec((GQA_KV_HEADS, tm, HEAD_DIM), lambda i: (0, i, 0)),
                   pl.BlockSpec((GQA_KV_HEADS, HEAD_DIM, tm), lambda i: (0, 0, i))],
        out_shape=[jax.ShapeDtypeStruct((GQA_HEADS, HEAD_DIM, N), jnp.bfloat16),
                   jax.ShapeDtypeStruct((GQA_KV_HEADS, N, HEAD_DIM), jnp.bfloat16),
                   jax.ShapeDtypeStruct((GQA_KV_HEADS, HEAD_DIM, N), jnp.bfloat16)],
        compiler_params=_cparams(("arbitrary",)),
        name="gqa_prep",
    )(p, p, p, q_gain_col, k_gain, *tabs, *tabs_t)


KEY_CHUNK = 256


def _sublane_fold(x, op):
    return op(x.reshape(x.shape[0] // 8, 8, x.shape[1]), axis=0)


def _flash_kernel(group, q_tiles, tks, qt_ref, k_ref, vt_ref, o_ref, s_even, s_odd, m_ref, l_ref, acc_ref):
    dv = vt_ref.shape[1]
    tq = ROW_TILE
    n_chunks = tks // KEY_CHUNK
    n_super = k_ref.shape[1] // tks
    streams = [(g, t) for g in range(group) for t in range(q_tiles)]
    m_ref[...] = jnp.full(m_ref.shape, -jnp.inf, jnp.float32)
    l_ref[...] = jnp.zeros(l_ref.shape, jnp.float32)
    acc_ref[...] = jnp.zeros(acc_ref.shape, jnp.float32)

    def stage(j_a, buf_a, j_b, buf_b, cmax_b):
        if j_a is not None:
            ka = pl.multiple_of(j_a * tks, tks)
            cmax_a = [None] * len(streams)
        if j_b is not None:
            kb = pl.multiple_of(j_b * tks, tks)
            m_old = [m_ref[s] for s in range(len(streams))]
            m_new = [jnp.maximum(m_old[s], jnp.max(cmax_b[s], axis=0, keepdims=True))
                     for s in range(len(streams))]
            lsum = [jnp.zeros((8, tq), jnp.float32) for _ in streams]
            pv = [None] * len(streams)
        for c in range(n_chunks):
            rows = slice(c * KEY_CHUNK, (c + 1) * KEY_CHUNK)
            if j_a is not None:
                k_c = k_ref[0, pl.ds(ka + c * KEY_CHUNK, KEY_CHUNK), :]
            if j_b is not None:
                vt_c = vt_ref[0, :, pl.ds(kb + c * KEY_CHUNK, KEY_CHUNK)]
            for s, (g, t) in enumerate(streams):
                if j_a is not None:
                    sc = _dot(k_c, qt_ref[g, :, t * tq:(t + 1) * tq])
                    buf_a[s, rows, :] = sc
                    cm = _sublane_fold(sc, jnp.max)
                    cmax_a[s] = cm if cmax_a[s] is None else jnp.maximum(cmax_a[s], cm)
                if j_b is not None:
                    p = jnp.exp2(buf_b[s, rows, :] - m_new[s])
                    lsum[s] = lsum[s] + _sublane_fold(p, jnp.sum)
                    d = _dot(vt_c, p.astype(jnp.bfloat16))
                    pv[s] = d if pv[s] is None else d + pv[s]
        if j_b is not None:
            for s in range(len(streams)):
                alpha = jnp.exp2(m_old[s] - m_new[s])
                l_ref[s] = alpha * l_ref[s] + jnp.sum(lsum[s], axis=0, keepdims=True)
                acc_ref[s] = alpha * acc_ref[s] + pv[s]
                m_ref[s] = m_new[s]
        return tuple(cmax_a) if j_a is not None else None

    def pair(i, cm):
        j = 2 * i + 1
        cm = stage(j, s_odd, j - 1, s_even, cm)
        return stage(j + 1, s_even, j, s_odd, cm)

    cmax = stage(0, s_even, None, None, None)
    cmax = lax.fori_loop(0, (n_super - 1) // 2, pair, cmax)
    if n_super % 2 == 0:
        cmax = stage(n_super - 1, s_odd, n_super - 2, s_even, cmax)
    stage(None, None, n_super - 1, s_odd if n_super % 2 == 0 else s_even, cmax)
    for s, (g, t) in enumerate(streams):
        o_ref[t * tq:(t + 1) * tq, g * dv:(g + 1) * dv] = (acc_ref[s] / l_ref[s]).T.astype(o_ref.dtype)


def flash_attention(qt, k, vt, *, group, q_tiles, tks, q_row0, n_q, k_row0, n_k):
    hk, _, dq = k.shape
    dv = vt.shape[1]
    tq = ROW_TILE * q_tiles
    assert q_row0 % tq == 0 and n_q % tq == 0 and k_row0 % n_k == 0 and n_k % tks == 0
    assert tks % KEY_CHUNK == 0
    qb, kb = q_row0 // tq, k_row0 // n_k
    n_str = group * q_tiles
    return pl.pallas_call(
        functools.partial(_flash_kernel, group, q_tiles, tks),
        grid=(hk, n_q // tq),
        in_specs=[pl.BlockSpec((group, dq, tq), lambda h, i: (h, 0, qb + i)),
                  pl.BlockSpec((1, n_k, dq), lambda h, i: (h, kb, 0)),
                  pl.BlockSpec((1, dv, n_k), lambda h, i: (h, 0, kb))],
        out_specs=pl.BlockSpec((tq, group * dv), lambda h, i: (i, h)),
        out_shape=jax.ShapeDtypeStruct((n_q, hk * group * dv), jnp.bfloat16),
        scratch_shapes=[pltpu.VMEM((n_str, tks, ROW_TILE), jnp.float32),
                        pltpu.VMEM((n_str, tks, ROW_TILE), jnp.float32),
                        pltpu.VMEM((n_str, 1, ROW_TILE), jnp.float32),
                        pltpu.VMEM((n_str, 1, ROW_TILE), jnp.float32),
                        pltpu.VMEM((n_str, dv, ROW_TILE), jnp.float32)],
        compiler_params=_cparams(("arbitrary", "arbitrary")),
        name="flash_attention",
    )(qt, k, vt)


def _gla_block(q_ref, k_ref, v_ref, c_ref, o_ref, s_ref, ge_ref, bd_ref, r0, forward):
    nb = GLA_BLOCK
    rows = pl.ds(r0, nb)
    q, k, v, b = q_ref[rows, :], k_ref[rows, :], v_ref[rows, :], c_ref[rows, :]
    total = b[nb - 1:nb, :] if forward else b[0:1, :]
    st = s_ref[...]
    qe = (q * jnp.exp(b)).astype(jnp.bfloat16)
    o = _dot_nt(qe, st.astype(jnp.bfloat16))
    rid = lax.broadcasted_iota(jnp.int32, b.shape, 0)
    terms = []
    for j in range(nb):
        keep = (rid >= j) if forward else (rid <= j)
        w = jnp.exp(jnp.where(keep, b - b[j:j + 1, :], -jnp.inf))
        terms.append((q * w * k[j:j + 1, :]).astype(jnp.bfloat16))
    scores = _dot(jnp.concatenate(terms, axis=0), ge_ref[...])
    for j in range(nb):
        o = o + scores[j * nb:(j + 1) * nb, :] * v[j:j + 1, :]
    o_ref[rows, :] = o
    ke = (k * jnp.exp(total - b)).astype(jnp.bfloat16)
    upd = lax.dot_general(v.astype(jnp.bfloat16), ke, (((0,), (0,)), ((), ())),
                          preferred_element_type=jnp.float32)
    s_ref[...] = jnp.exp(total) * st + bd_ref[...] * upd


def _gla_kernel(qf, kf, vf, mf, qb, kb, vb, mb, wgf, bgf, wgb, bgb, tri_f, tri_b, ge, bd,
                of, ob, cf_ref, cb_ref, sf_ref, sb_ref):
    @pl.when(pl.program_id(0) == 0)
    def _():
        sf_ref[...] = jnp.zeros(sf_ref.shape, jnp.float32)
        sb_ref[...] = jnp.zeros(sb_ref.shape, jnp.float32)

    def cum_log_decay(m_ref, w_ref, b_ref, tri_ref):
        z = jnp.dot(m_ref[...], w_ref[...], precision=lax.Precision.HIGHEST,
                    preferred_element_type=jnp.float32) + b_ref[...]
        la = (jnp.minimum(z, 0.0) - jnp.log1p(jnp.exp(-jnp.abs(z)))) * (1.0 / GLA_GATE_NORM)
        return jnp.dot(tri_ref[...], la, precision=lax.Precision.HIGHEST,
                       preferred_element_type=jnp.float32)

    cf_ref[...] = cum_log_decay(mf, wgf, bgf, tri_f)
    cb_ref[...] = cum_log_decay(mb, wgb, bgb, tri_b)
    n_blocks = qf.shape[0] // GLA_BLOCK

    def body(t, carry):
        r_f = pl.multiple_of(t * GLA_BLOCK, GLA_BLOCK)
        r_b = pl.multiple_of((n_blocks - 1 - t) * GLA_BLOCK, GLA_BLOCK)
        _gla_block(qf, kf, vf, cf_ref, of, sf_ref, ge, bd, r_f, True)
        _gla_block(qb, kb, vb, cb_ref, ob, sb_ref, ge, bd, r_b, False)
        return carry

    lax.fori_loop(0, n_blocks, body, 0)


def gla_scan(p, w_gf_p, b_gf, w_gb_p, b_gb, consts):
    N = p.shape[0]
    tm = ROW_TILE
    nt = N // tm
    nk, nv = GLA_HEADS * GLA_DK, GLA_HEADS * GLA_DV
    fwd = lambda i: (i + nt - 1) % nt
    bwd = lambda i: (2 * nt - 1 - i) % nt
    col = lambda w, c, order: pl.BlockSpec((tm, w), lambda i: (order(i), c))
    full = lambda a: pl.BlockSpec(a.shape, lambda i: (0,) * a.ndim)
    side = lambda order: [col(nk, COL_LQ // nk, order), col(nk, COL_LK // nk, order),
                          col(nv, COL_LV // nv, order), col(LANES, COL_MISC // LANES, order)]
    small = (w_gf_p, b_gf, w_gb_p, b_gb) + tuple(consts)
    return pl.pallas_call(
        _gla_kernel,
        grid=(nt,),
        in_specs=side(fwd) + side(bwd) + [full(a) for a in small],
        out_specs=[pl.BlockSpec((tm, nv), lambda i: (fwd(i), 0)),
                   pl.BlockSpec((tm, nv), lambda i: (bwd(i), 0))],
        out_shape=[jax.ShapeDtypeStruct((N, nv), jnp.float32)] * 2,
        scratch_shapes=[pltpu.VMEM((tm, nk), jnp.float32), pltpu.VMEM((tm, nk), jnp.float32),
                        pltpu.VMEM((nv, nk), jnp.float32), pltpu.VMEM((nv, nk), jnp.float32)],
        compiler_params=_cparams(("arbitrary",)),
        name="gla_scan",
    )(p, p, p, p, p, p, p, p, *small)


def _gla_consts():
    i = jnp.arange(ROW_TILE)
    same = (i[:, None] // GLA_BLOCK) == (i[None, :] // GLA_BLOCK)
    tri_f = (same & (i[None, :] <= i[:, None])).astype(jnp.float32)
    tri_b = (same & (i[None, :] >= i[:, None])).astype(jnp.float32)
    hk = jnp.arange(GLA_HEADS * GLA_DK) // GLA_DK
    hv = jnp.arange(GLA_HEADS * GLA_DV) // GLA_DV
    ge = (hk[:, None] == hv[None, :]).astype(jnp.bfloat16)
    bd = (hv[:, None] == hk[None, :]).astype(jnp.float32)
    return tri_f, tri_b, ge, bd


def _prep_gate(w_f, b_f, w_b, b_b):
    z = jnp.zeros((LANES, w_f.shape[1]), jnp.float32)
    return (z.at[MLA_ROPE:MLA_ROPE + GLA_GATE_RANK].set(w_f), b_f[None],
            z.at[MLA_ROPE + GLA_GATE_RANK:MLA_ROPE + 2 * GLA_GATE_RANK].set(w_b), b_b[None])


def _layer_norm(u, g, b):
    mu = jnp.mean(u, axis=-1, keepdims=True)
    d = u - mu
    var = jnp.mean(d * d, axis=-1, keepdims=True)
    return d * lax.rsqrt(var + LN_EPS) * g + b


def _first_max(rows):
    best, idx = rows[0], jnp.zeros(rows[0].shape, jnp.int32)
    for i in range(1, len(rows)):
        better = rows[i] > best
        idx = jnp.where(better, i, idx)
        best = jnp.where(better, rows[i], best)
    return best, idx


def _route(lt):
    grp_rows = [lt[g:g + 1, :] for g in range(N_GROUPS)]
    g_best, grp = _first_max(grp_rows)
    grp_w = 1.0 / sum(jnp.exp(r - g_best) for r in grp_rows)
    in_grp = []
    for j in range(EXPERTS_PER_GROUP):
        val = lt[8 + j:9 + j, :]
        for g in range(1, N_GROUPS):
            r = 8 + g * EXPERTS_PER_GROUP + j
            val = jnp.where(grp == g, lt[r:r + 1, :], val)
        in_grp.append(val)
    v1, i1 = _first_max(in_grp)
    rest = [jnp.where(i1 == j, -jnp.inf, in_grp[j]) for j in range(EXPERTS_PER_GROUP)]
    v2, i2 = _first_max(rest)
    e = jnp.exp(v2 - v1)
    w1 = 1.0 / (1.0 + e)
    ids = jnp.concatenate([grp * EXPERTS_PER_GROUP + i1, grp * EXPERTS_PER_GROUP + i2], axis=0)
    gates = jnp.concatenate([grp_w * w1, grp_w * (e * w1)], axis=0)
    return ids, gates


def _outproj_kernel(n_lat_tiles, alpha,
                    x_ref, yml, ymc, ygl, ygc, of_ref, ob_ref, g_ref, on_ref, wo_ref,
                    g1_ref, sh2_ref, sc2_ref, lg_ref, lb_ref, wr_ref, br_ref,
                    x1_ref, h2_ref, id_ref, gt_ref):
    is_ctx = pl.program_id(0) >= n_lat_tiles
    y_mla = jnp.where(is_ctx, ymc[...], yml[...])
    y_gqa = jnp.where(is_ctx, ygc[...], ygl[...])
    o = of_ref[...] + ob_ref[...]
    heads = [_rms(o[:, h * GLA_DV:(h + 1) * GLA_DV], on_ref[...]) for h in range(GLA_HEADS)]
    g = g_ref[...]
    y_gla = (jnp.concatenate(heads, axis=1) * (g * jax.nn.sigmoid(g))).astype(jnp.bfloat16)
    w_mla, w_gqa = y_mla.shape[1], y_gqa.shape[1]
    y = (_dot(y_mla, wo_ref[0:w_mla, :]) + _dot(y_gqa, wo_ref[w_mla:w_mla + w_gqa, :])
         + _dot(y_gla, wo_ref[w_mla + w_gqa:, :]))
    x1 = _layer_norm(alpha * x_ref[...] + _mod_row(g1_ref, is_ctx) * y, lg_ref[...], lb_ref[...])
    x1_ref[...] = x1
    h2 = x1 * (1.0 + _mod_row(sc2_ref, is_ctx)) + _mod_row(sh2_ref, is_ctx)
    h2_ref[...] = h2
    lt = lax.dot_general(wr_ref[...], h2, (((1,), (1,)), ((), ())), precision=lax.Precision.HIGHEST,
                         preferred_element_type=jnp.float32) + br_ref[...]
    ids, gates = _route(lt)
    rid = lax.broadcasted_iota(jnp.int32, id_ref.shape, 0)
    id_ref[...] = jnp.where(rid == 0, ids[0:1], jnp.where(rid == 1, ids[1:2], 0))
    gt_ref[...] = jnp.where(rid == 0, gates[0:1], jnp.where(rid == 1, gates[1:2], 0.0))


def out_projection(xs, y_mla, y_gqa, o_f, o_b, p, out_norm, w_out, mod_l, ln_g, ln_b, w_rt, b_rt,
                   n_lat, alpha):
    N, D = xs.shape
    tm = ROW_TILE
    nl = n_lat // tm
    assert N - n_lat == tm
    nv = GLA_HEADS * GLA_DV
    lat = lambda a: pl.BlockSpec((tm, a.shape[1]), lambda i: (jnp.minimum(i, nl - 1), 0))
    ctx = lambda a: pl.BlockSpec((tm, a.shape[1]), lambda i: (0, 0))
    row = lambda w, c: pl.BlockSpec((tm, w), lambda i: (i, c))
    full = lambda a: pl.BlockSpec(a.shape, lambda i: (0,) * a.ndim)
    modc = lambda c: pl.BlockSpec((8, D), lambda i: (0, c))
    (yml, ymc), (ygl, ygc) = y_mla, y_gqa
    return pl.pallas_call(
        functools.partial(_outproj_kernel, nl, alpha),
        grid=(N // tm,),
        in_specs=[row(D, 0), lat(yml), ctx(ymc), lat(ygl), ctx(ygc), row(nv, 0), row(nv, 0),
                  row(nv, COL_LG // nv), full(out_norm), full(w_out),
                  modc(2), modc(3), modc(4), full(ln_g), full(ln_b), full(w_rt), full(b_rt)],
        out_specs=[row(D, 0), row(D, 0), pl.BlockSpec((8, tm), lambda i: (0, i)),
                   pl.BlockSpec((8, tm), lambda i: (0, i))],
        out_shape=[jax.ShapeDtypeStruct((N, D), jnp.float32), jax.ShapeDtypeStruct((N, D), jnp.float32),
                   jax.ShapeDtypeStruct((8, N), jnp.int32), jax.ShapeDtypeStruct((8, N), jnp.float32)],
        compiler_params=_cparams(("arbitrary",)),
        name="out_projection",
    )(xs, yml, ymc, ygl, ygc, o_f, o_b, p, out_norm, w_out, mod_l, mod_l, mod_l, ln_g, ln_b, w_rt, b_rt)


def _row_copy(src_hbm, idx_ref, dst, sem, r):
    return pltpu.make_async_copy(src_hbm.at[pl.ds(idx_ref[0, 0, r], 1), :], dst.at[pl.ds(r, 1), :], sem)


def _start_rows(src_hbm, idx_ref, dst, sem, unrolled):
    n = dst.shape[0]
    if unrolled:
        for r in range(n):
            _row_copy(src_hbm, idx_ref, dst, sem, r).start()
    else:
        def start(r, c):
            _row_copy(src_hbm, idx_ref, dst, sem, r).start()
            return c
        lax.fori_loop(0, n, start, 0, unroll=8)


def _wait_rows(src_hbm, dst, sem):
    pltpu.make_async_copy(src_hbm.at[pl.ds(0, dst.shape[0]), :], dst, sem).wait()


def _double_buffered_gather(i, n_steps, src_hbm, idx_ref, idx_next_ref, bufs, sems, compute):
    @pl.when(i == 0)
    def _():
        _start_rows(src_hbm, idx_ref, bufs[0], sems.at[0], unrolled=False)

    for parity in (0, 1):
        cur, nxt = bufs[parity], bufs[1 - parity]
        for has_next in (True, False):
            more = i + 1 < n_steps
            cond = (i < n_steps) & (i % 2 == parity) & (more if has_next else jnp.logical_not(more))

            @pl.when(cond)
            def _(cur=cur, nxt=nxt, parity=parity, has_next=has_next):
                _wait_rows(src_hbm, cur, sems.at[parity])
                if has_next:
                    _start_rows(src_hbm, idx_next_ref, nxt, sems.at[1 - parity], unrolled=True)
                compute(cur)


def _expert_kernel(te_ref, nu_ref, idx_ref, idx_next_ref, h_hbm, wg_ref, wu_ref, wd_ref, y_ref,
                   xa, xb, wg_bf, wu_bf, wd_bf, sems):
    i = pl.program_id(0)
    n_used = nu_ref[0]

    @pl.when((i < n_used) & ((i == 0) | (te_ref[i] != te_ref[jnp.maximum(i - 1, 0)])))
    def _():
        wg_bf[...] = wg_ref[0].astype(jnp.bfloat16)
        wu_bf[...] = wu_ref[0].astype(jnp.bfloat16)
        wd_bf[...] = wd_ref[0].astype(jnp.bfloat16)

    def compute(x_ref):
        x = x_ref[...].astype(jnp.bfloat16)
        a = _dot(x, wg_bf[...])
        u = _dot(x, wu_bf[...])
        hid = (a * jax.nn.sigmoid(a) * u).astype(jnp.bfloat16)
        y_ref[...] = _dot(hid, wd_bf[...])

    _double_buffered_gather(i, n_used, h_hbm, idx_ref, idx_next_ref, (xa, xb), sems, compute)

    @pl.when(i >= n_used)
    def _():
        y_ref[...] = jnp.zeros(y_ref.shape, jnp.float32)


def expert_ffn(h2, tile_expert, n_used, sorted_tok, w_g, w_u, w_d):
    N, D = h2.shape
    tm = MOE_TILE
    n_tiles = sorted_tok.shape[0] // tm
    E, _, F = w_g.shape
    bf = jnp.bfloat16
    idx = sorted_tok.reshape(n_tiles, 1, tm)
    nxt = lambda i, te, nu: (jnp.minimum(i + 1, n_tiles - 1), 0, 0)
    gs = pltpu.PrefetchScalarGridSpec(
        num_scalar_prefetch=2,
        grid=(n_tiles,),
        in_specs=[pl.BlockSpec((1, 1, tm), lambda i, te, nu: (i, 0, 0), memory_space=pltpu.SMEM),
                  pl.BlockSpec((1, 1, tm), nxt, memory_space=pltpu.SMEM),
                  pl.BlockSpec(memory_space=pl.ANY),
                  pl.BlockSpec((1, D, F), lambda i, te, nu: (te[i], 0, 0)),
                  pl.BlockSpec((1, D, F), lambda i, te, nu: (te[i], 0, 0)),
                  pl.BlockSpec((1, F, D), lambda i, te, nu: (te[i], 0, 0))],
        out_specs=pl.BlockSpec((tm, D), lambda i, te, nu: (i, 0)),
        scratch_shapes=[pltpu.VMEM((tm, D), jnp.float32), pltpu.VMEM((tm, D), jnp.float32),
                        pltpu.VMEM((D, F), bf), pltpu.VMEM((D, F), bf), pltpu.VMEM((F, D), bf),
                        pltpu.SemaphoreType.DMA((2,))],
    )
    return pl.pallas_call(
        _expert_kernel,
        grid_spec=gs,
        out_shape=jax.ShapeDtypeStruct((n_tiles * tm, D), jnp.float32),
        compiler_params=_cparams(("arbitrary",)),
        name="expert_ffn",
    )(tile_expert, n_used, idx, idx, h2, w_g, w_u, w_d)


def _combine_kernel(n_lat_tiles, alpha, pos_ref, pos_next_ref, y_hbm, x_ref, gc_ref, g2_ref, lg_ref,
                    lb_ref, o_ref, ya, yb, sems):
    i = pl.program_id(0)
    is_ctx = i >= n_lat_tiles
    tm = x_ref.shape[0]

    def compute(y_ref):
        gc = gc_ref[...]
        f = gc[:, 0:1] * y_ref[0:tm, :] + gc[:, 1:2] * y_ref[tm:2 * tm, :]
        o_ref[...] = _layer_norm(alpha * x_ref[...] + _mod_row(g2_ref, is_ctx) * f, lg_ref[...], lb_ref[...])

    _double_buffered_gather(i, pl.num_programs(0), y_hbm, pos_ref, pos_next_ref, (ya, yb), sems, compute)


def moe_combine(xs1, y_sorted, pos, gate_cols, mod_l, ln_g, ln_b, n_lat, alpha):
    N, D = xs1.shape
    tm = ROW_TILE
    nt = N // tm
    full = lambda a: pl.BlockSpec(a.shape, lambda i: (0,) * a.ndim)
    return pl.pallas_call(
        functools.partial(_combine_kernel, n_lat // tm, alpha),
        grid=(nt,),
        in_specs=[pl.BlockSpec((1, 1, 2 * tm), lambda i: (i, 0, 0), memory_space=pltpu.SMEM),
                  pl.BlockSpec((1, 1, 2 * tm), lambda i: (jnp.minimum(i + 1, nt - 1), 0, 0),
                               memory_space=pltpu.SMEM),
                  pl.BlockSpec(memory_space=pl.ANY),
                  pl.BlockSpec((tm, D), lambda i: (i, 0)),
                  pl.BlockSpec((tm, 2), lambda i: (i, 0)),
                  pl.BlockSpec((8, D), lambda i: (0, 5)), full(ln_g), full(ln_b)],
        out_specs=pl.BlockSpec((tm, D), lambda i: (i, 0)),
        out_shape=jax.ShapeDtypeStruct((N, D), jnp.float32),
        scratch_shapes=[pltpu.VMEM((2 * tm, D), jnp.float32), pltpu.VMEM((2 * tm, D), jnp.float32),
                        pltpu.SemaphoreType.DMA((2,))],
        compiler_params=_cparams(("arbitrary",)),
        name="moe_combine",
    )(pos, pos, y_sorted, xs1, gate_cols, mod_l, ln_g, ln_b)


def _dispatch_plan(ids, n_tok):
    tm = MOE_TILE
    n_asg = 2 * n_tok
    assert n_asg * n_asg < 2 ** 31
    n_tiles = n_asg // tm + N_EXPERTS
    i32 = jnp.int32
    e_flat = ids.reshape(n_asg)
    ar = jnp.arange(n_asg, dtype=i32)
    order = jnp.sort(e_flat * n_asg + ar) % n_asg
    rank = jnp.sort(order * n_asg + ar) % n_asg
    experts = jnp.arange(N_EXPERTS, dtype=i32)
    onehot = e_flat[:, None] == experts[None, :]
    counts = jnp.sum(onehot, axis=0, dtype=i32)
    ends = jnp.cumsum(counts).astype(i32)
    padded = ((counts + tm - 1) // tm) * tm
    pad_ends = jnp.cumsum(padded).astype(i32)
    shift = (pad_ends - padded) - (ends - counts)
    pos = rank + jnp.sum(jnp.where(onehot, shift[None, :], 0), axis=1, dtype=i32)
    n_used = (pad_ends[-1] // tm).reshape(1)
    tile_start = jnp.arange(n_tiles, dtype=i32) * tm
    tile_expert = jnp.sum(tile_start[:, None] >= pad_ends[None, :], axis=1, dtype=i32)
    last = jnp.sum((pad_ends[-1] - 1) >= pad_ends, dtype=i32)
    tile_expert = jnp.minimum(tile_expert, last)
    of_tile = tile_expert[:, None] == experts[None, :]
    shift_t = jnp.sum(jnp.where(of_tile, shift[None, :], 0), axis=1, dtype=i32)
    ends_t = jnp.sum(jnp.where(of_tile, ends[None, :], 0), axis=1, dtype=i32)
    r = tile_start[:, None] + jnp.arange(tm, dtype=i32)[None, :] - shift_t[:, None]
    valid = (r < ends_t[:, None]) & (tile_start[:, None] < pad_ends[-1])
    sorted_tok = jnp.where(valid, order[jnp.clip(r, 0, n_asg - 1)] % n_tok, 0).reshape(n_tiles * tm)
    nt = n_tok // ROW_TILE
    pos_tiles = jnp.concatenate([pos[:n_tok].reshape(nt, 1, ROW_TILE),
                                 pos[n_tok:].reshape(nt, 1, ROW_TILE)], axis=2)
    return sorted_tok, tile_expert, n_used, pos_tiles


def _rope_tables(n_lat, n_ctx, dim):
    rows = n_lat // GRID_W
    row = jnp.repeat(jnp.arange(rows), GRID_W).astype(jnp.float32)
    col = jnp.tile(jnp.arange(GRID_W), rows).astype(jnp.float32)
    half = dim // 2
    inv_freq = ROPE_THETA ** (-jnp.arange(0, half, 2, dtype=jnp.float32) / half)
    ang_r = row[:, None] * inv_freq
    ang_c = col[:, None] * inv_freq
    ang = jnp.concatenate([ang_r, ang_r, ang_c, ang_c], axis=-1)
    cos, sin = jnp.cos(ang), jnp.sin(ang)
    q = dim // 4
    lane = jnp.arange(dim)
    first = ((lane // q) % 2) == 0
    sin_a = jnp.where(first, -sin, 0.0)
    sin_b = jnp.where(first, 0.0, sin)

    def pad(t, fill):
        t = jnp.pad(t, ((0, 0), (0, LANES - dim)), constant_values=fill)
        return jnp.pad(t, ((0, n_ctx), (0, 0)), constant_values=fill)

    return pad(cos, 1.0), pad(sin_a, 0.0), pad(sin_b, 0.0)


def _in_perm():
    a = MLA_Q_RANK + MLA_KV_RANK + MLA_ROPE
    b = a + (GQA_HEADS + 2 * GQA_KV_HEADS) * HEAD_DIM
    nk, nv = GLA_HEADS * GLA_DK, GLA_HEADS * GLA_DV
    r = lambda s, n: list(range(s, s + n))
    cols = (r(0, MLA_Q_RANK + MLA_KV_RANK)
            + r(a, b - a)
            + r(b, 2 * nk + 2 * nv)
            + r(MLA_Q_RANK + MLA_KV_RANK, MLA_ROPE)
            + r(b + 2 * nk + 2 * nv, 2 * GLA_GATE_RANK))
    return jnp.asarray(cols, jnp.int32)


def _prep_w_in(w_in_l):
    w = w_in_l[:, _in_perm()]
    scale = jnp.ones((w.shape[1],), jnp.float32).at[COL_LQ:COL_LK].set(GLA_DK ** -0.5)
    w = w * scale
    return jnp.pad(w, ((0, 0), (0, P_COLS - w.shape[1]))).astype(jnp.bfloat16)


def _prep_w_uq(w_uq_l):
    r = w_uq_l.shape[0]
    w = w_uq_l.reshape(r, MLA_HEADS, MLA_NOPE + MLA_ROPE)
    w = jnp.pad(w, ((0, 0), (0, 0), (0, MLA_QPAD - MLA_NOPE - MLA_ROPE)))
    return w.reshape(r, MLA_HEADS * MLA_QPAD).astype(jnp.bfloat16)


def _prep_router(w_rg, b_rg, w_re, b_re):
    D = w_rg.shape[0]
    w = jnp.zeros((LANES, D), jnp.float32).at[0:N_GROUPS].set(w_rg.T).at[8:8 + N_EXPERTS].set(w_re.T)
    b = jnp.zeros((LANES,), jnp.float32).at[0:N_GROUPS].set(b_rg).at[8:8 + N_EXPERTS].set(b_re)
    return w, b[:, None]


def _kv_tile(n):
    return max(t for t in (256, 640, 1280) if n % t == 0)


def kernel(x, c, ctx, c_ctx, ada_w, ada_b, w_in, mla_q_norm, mla_w_uq, mla_kv_norm, mla_w_ukv, gqa_q_norm, gqa_k_norm, gla_w_gate_fwd, gla_b_gate_fwd, gla_w_gate_bwd, gla_b_gate_bwd, gla_out_norm, w_out, ln1_g, ln1_b, w_route_group, b_route_group, w_route_expert, b_route_expert, w_expert_gate, w_expert_up, w_expert_down, ln2_g, ln2_b):
    T, C = x.shape[1], ctx.shape[1]
    N = T + C
    L = ada_w.shape[0]
    assert x.shape[0] == 1 and C == ROW_TILE and T % ROW_TILE == 0 and T % GRID_W == 0
    alpha = (2.0 * L) ** 0.25
    bf = jnp.bfloat16

    xs = jnp.concatenate([x[0], ctx[0]], axis=0)
    mods = modulation(c, c_ctx, ada_w, ada_b)
    tabs_mla = _rope_tables(T, C, MLA_ROPE)
    tabs_gqa = _rope_tables(T, C, HEAD_DIM)
    gla_consts = _gla_consts()
    mla_scale = (MLA_NOPE + MLA_ROPE) ** -0.5 * LOG2E
    gqa_scale = HEAD_DIM ** -0.5 * LOG2E
    tabs_mla_t = tuple(t.T for t in tabs_mla)
    tabs_gqa_t = tuple(t.T for t in tabs_gqa)
    lat = dict(q_row0=0, n_q=T, k_row0=0, n_k=N, tks=_kv_tile(N))
    con = dict(q_row0=T, n_q=C, k_row0=T, n_k=C, tks=C)

    for l in range(L):
        mod_l = mods[l]
        p = in_projection(xs, mod_l, _prep_w_in(w_in[l]), T)

        w_ukv = mla_w_ukv[l].reshape(MLA_KV_RANK, MLA_HEADS, MLA_NOPE + MLA_V)
        w_kn = w_ukv[:, :, :MLA_NOPE].reshape(MLA_KV_RANK, MLA_HEADS * MLA_NOPE).astype(bf)
        w_v_t = w_ukv[:, :, MLA_NOPE:].reshape(MLA_KV_RANK, MLA_HEADS * MLA_V).T.astype(bf)
        qt, k, vt = mla_prep(p, (mla_q_norm[l] * mla_scale)[None], _prep_w_uq(mla_w_uq[l]).T,
                             mla_kv_norm[l][None], w_kn, w_v_t, tabs_mla, tabs_mla_t)
        y_mla = (flash_attention(qt, k, vt, group=1, q_tiles=2 if T % (2 * ROW_TILE) == 0 else 1, **lat),
                 flash_attention(qt, k, vt, group=1, q_tiles=1, **con))
        qt, k, vt = gqa_prep(p, (gqa_q_norm[l] * gqa_scale)[:, None], gqa_k_norm[l][None],
                             tabs_gqa, tabs_gqa_t)
        y_gqa = (flash_attention(qt, k, vt, group=GQA_GROUP, q_tiles=1, **lat),
                 flash_attention(qt, k, vt, group=GQA_GROUP, q_tiles=1, **con))
        o_f, o_b = gla_scan(p, *_prep_gate(gla_w_gate_fwd[l], gla_b_gate_fwd[l],
                                           gla_w_gate_bwd[l], gla_b_gate_bwd[l]), gla_consts)

        w_rt, b_rt = _prep_router(w_route_group[l], b_route_group[l], w_route_expert[l], b_route_expert[l])
        x1, h2, ids, gates = out_projection(xs, y_mla, y_gqa, o_f, o_b, p, gla_out_norm[l][None],
                                            w_out[l].astype(bf), mod_l, ln1_g[l][None], ln1_b[l][None],
                                            w_rt, b_rt, T, alpha)

        sorted_tok, tile_expert, n_used, pos_tiles = _dispatch_plan(ids[0:2], N)
        y_sorted = expert_ffn(h2, tile_expert, n_used, sorted_tok, w_expert_gate[l], w_expert_up[l],
                              w_expert_down[l])
        xs = moe_combine(x1, y_sorted, pos_tiles, gates[0:2].T, mod_l, ln2_g[l][None], ln2_b[l][None],
                         T, alpha)
    return xs[:T][None]
```

```python
import functools
import math

import jax
import jax.numpy as jnp
from jax import lax
from jax.experimental import pallas as pl
from jax.experimental.pallas import tpu as pltpu

GRID_W = 64
ROPE_THETA = 10000.0
NORM_EPS = 1e-6
LN_EPS = 1e-5
HEAD_DIM = 128
MLA_HEADS = 6
MLA_Q_RANK = 512
MLA_KV_RANK = 256
MLA_NOPE = 128
MLA_ROPE = 64
MLA_V = 128
GQA_HEADS = 6
GQA_KV_HEADS = 2
GQA_GROUP = GQA_HEADS // GQA_KV_HEADS
GLA_HEADS = 4
GLA_DK = 64
GLA_DV = 128
GLA_GATE_RANK = 16
GLA_GATE_NORM = 16.0
N_GROUPS = 4
EXPERTS_PER_GROUP = 8
N_EXPERTS = N_GROUPS * EXPERTS_PER_GROUP
N_MOD = 6

LANES = 128
VMEM_LIMIT = 48 * 1024 * 1024

ROW_TILE = 256
GLA_BLOCK = 16
MOE_TILE = 256

COL_CQ = 0
COL_CKV = 512
COL_GQ = 768
COL_GK = 1536
COL_GV = 1792
COL_LQ = 2048
COL_LK = 2304
COL_LV = 2560
COL_LG = 3072
COL_MISC = 3584
P_COLS = 3840
P_HALF = P_COLS // 2
MLA_QPAD = 256

LOG2E = math.log2(math.e)


def _cparams(sem):
    return pltpu.CompilerParams(dimension_semantics=sem, vmem_limit_bytes=VMEM_LIMIT)


def _dot(a, b):
    return jnp.dot(a, b, preferred_element_type=jnp.float32)


def _dot_nt(a, b):
    return lax.dot_general(a, b, (((1,), (1,)), ((), ())), preferred_element_type=jnp.float32)


def _split_bf16(x):
    hi = x.astype(jnp.bfloat16)
    return hi, (x - hi.astype(jnp.float32)).astype(jnp.bfloat16)


def _dot_split(a, b, dot=_dot):
    a_hi, a_lo = _split_bf16(a)
    b_hi, b_lo = _split_bf16(b)
    return dot(a_hi, b_hi) + dot(a_lo, b_hi) + dot(a_hi, b_lo)


def _mod_kernel(crep_ref, w_ref, b_ref, o_ref):
    tn = w_ref.shape[2]
    rows = []
    for r in range(2):
        cr = crep_ref[r]
        a = cr * jax.nn.sigmoid(cr)
        parts = []
        for j in range(tn // LANES):
            wj = w_ref[0, :, j * LANES:(j + 1) * LANES]
            parts.append(jnp.sum(wj * a, axis=0, keepdims=True))
        rows.append(jnp.concatenate(parts, axis=1) + b_ref[0])
    rid = lax.broadcasted_iota(jnp.int32, (8, tn), 0)
    o_ref[0] = jnp.where(rid == 0, rows[0], jnp.where(rid == 1, rows[1], 0.0))


def modulation(c, c_ctx, ada_w, ada_b):
    L, D, M = ada_w.shape
    tn = max(t for t in range(LANES, 1024 + 1, LANES) if M % t == 0)
    crep = jnp.broadcast_to(jnp.stack([c[0], c_ctx])[:, :, None], (2, D, LANES))
    return pl.pallas_call(
        _mod_kernel,
        grid=(L, M // tn),
        in_specs=[pl.BlockSpec((2, D, LANES), lambda l, j: (0, 0, 0)),
                  pl.BlockSpec((1, D, tn), lambda l, j: (l, 0, j)),
                  pl.BlockSpec((1, 1, tn), lambda l, j: (l, 0, j))],
        out_specs=pl.BlockSpec((1, 8, tn), lambda l, j: (l, 0, j)),
        out_shape=jax.ShapeDtypeStruct((L, 8, M), jnp.float32),
        compiler_params=_cparams(("arbitrary", "arbitrary")),
        name="modulation",
    )(crep, ada_w, ada_b.reshape(L, 1, M))


def _mod_row(ref, is_ctx):
    return jnp.where(is_ctx, ref[1:2, :], ref[0:1, :])


def _inproj_kernel(n_lat_tiles, x_ref, sh_ref, sc_ref, w_ref, o_ref):
    is_ctx = pl.program_id(1) >= n_lat_tiles
    h = x_ref[...] * (1.0 + _mod_row(sc_ref, is_ctx)) + _mod_row(sh_ref, is_ctx)
    o_ref[...] = _dot(h.astype(jnp.bfloat16), w_ref[...])


def in_projection(xs, mod_l, w_in_p, n_lat):
    N, D = xs.shape
    tm = ROW_TILE
    return pl.pallas_call(
        functools.partial(_inproj_kernel, n_lat // tm),
        grid=(2, N // tm),
        in_specs=[pl.BlockSpec((tm, D), lambda j, i: (i, 0)),
                  pl.BlockSpec((8, D), lambda j, i: (0, 0)),
                  pl.BlockSpec((8, D), lambda j, i: (0, 1)),
                  pl.BlockSpec((D, P_HALF), lambda j, i: (0, j))],
        out_specs=pl.BlockSpec((tm, P_HALF), lambda j, i: (i, j)),
        out_shape=jax.ShapeDtypeStruct((N, P_COLS), jnp.float32),
        compiler_params=_cparams(("arbitrary", "arbitrary")),
        name="in_projection",
    )(xs, mod_l, mod_l, w_in_p)


def _rope(x, cos, sin_a, sin_b, quarter):
    w = x.shape[-1]
    return x * cos + pltpu.roll(x, w - quarter, 1) * sin_a + pltpu.roll(x, quarter, 1) * sin_b


def _rms(x, gain):
    return x * lax.rsqrt(jnp.mean(x * x, axis=-1, keepdims=True) + NORM_EPS) * gain


def _rope_t(x, cos, sin_a, sin_b, quarter):
    w = x.shape[0]
    return x * cos + pltpu.roll(x, w - quarter, 0) * sin_a + pltpu.roll(x, quarter, 0) * sin_b


def _mla_prep_kernel(cq_ref, ckv_ref, misc_ref, qn_ref, wuqt_ref, kvn_ref, wkn_ref, wvt_ref,
                     cos_ref, sa_ref, sb_ref, cost_ref, sat_ref, sbt_ref, qt_ref, k_ref, vt_ref):
    cq = _rms(cq_ref[...], qn_ref[...]).astype(jnp.bfloat16)
    qt_all = _dot_nt(wuqt_ref[...], cq)
    ckv = _rms(ckv_ref[...], kvn_ref[...]).astype(jnp.bfloat16)
    kn_all = _dot(ckv, wkn_ref[...])
    vt_all = _dot_nt(wvt_ref[...], ckv)
    lane = lax.broadcasted_iota(jnp.int32, misc_ref.shape, 1)
    k_rope = _rope(misc_ref[...], cos_ref[...], sa_ref[...], sb_ref[...], MLA_ROPE // 4)
    k_rope = jnp.where(lane < MLA_ROPE, k_rope, 0.0).astype(jnp.bfloat16)
    cost, sat, sbt = cost_ref[...], sat_ref[...], sbt_ref[...]
    for h in range(MLA_HEADS):
        b0 = h * MLA_QPAD
        qt_ref[h, 0:LANES, :] = qt_all[b0:b0 + LANES, :].astype(jnp.bfloat16)
        qt_ref[h, LANES:2 * LANES, :] = _rope_t(qt_all[b0 + LANES:b0 + 2 * LANES, :], cost, sat, sbt,
                                                MLA_ROPE // 4).astype(jnp.bfloat16)
        k_ref[h, :, 0:LANES] = kn_all[:, h * MLA_NOPE:(h + 1) * MLA_NOPE].astype(jnp.bfloat16)
        k_ref[h, :, LANES:2 * LANES] = k_rope
        vt_ref[h] = vt_all[h * MLA_V:(h + 1) * MLA_V, :].astype(jnp.bfloat16)


def mla_prep(p, q_gain, w_uq_t, kv_gain, w_kn, w_v_t, tabs, tabs_t):
    N = p.shape[0]
    tm = ROW_TILE
    H = MLA_HEADS
    row = lambda w, c: pl.BlockSpec((tm, w), lambda i: (i, c))
    full = lambda a: pl.BlockSpec(a.shape, lambda i: (0,) * a.ndim)
    tab = pl.BlockSpec((tm, LANES), lambda i: (i, 0))
    tab_t = pl.BlockSpec((LANES, tm), lambda i: (0, i))
    return pl.pallas_call(
        _mla_prep_kernel,
        grid=(N // tm,),
        in_specs=[row(MLA_Q_RANK, COL_CQ // MLA_Q_RANK), row(MLA_KV_RANK, COL_CKV // MLA_KV_RANK),
                  row(LANES, COL_MISC // LANES),
                  full(q_gain), full(w_uq_t), full(kv_gain), full(w_kn), full(w_v_t),
                  tab, tab, tab, tab_t, tab_t, tab_t],
        out_specs=[pl.BlockSpec((H, MLA_QPAD, tm), lambda i: (0, 0, i)),
                   pl.BlockSpec((H, tm, MLA_QPAD), lambda i: (0, i, 0)),
                   pl.BlockSpec((H, MLA_V, tm), lambda i: (0, 0, i))],
        out_shape=[jax.ShapeDtypeStruct((H, MLA_QPAD, N), jnp.bfloat16),
                   jax.ShapeDtypeStruct((H, N, MLA_QPAD), jnp.bfloat16),
                   jax.ShapeDtypeStruct((H, MLA_V, N), jnp.bfloat16)],
        compiler_params=_cparams(("arbitrary",)),
        name="mla_prep",
    )(p, p, p, q_gain, w_uq_t, kv_gain, w_kn, w_v_t, *tabs, *tabs_t)


def _gqa_prep_kernel(q_in, k_in, v_in, qn_ref, kn_ref, cos_ref, sa_ref, sb_ref,
                     cost_ref, sat_ref, sbt_ref, qt_ref, k_ref, vt_ref):
    cos, sa, sb = cos_ref[...], sa_ref[...], sb_ref[...]
    cost, sat, sbt = cost_ref[...], sat_ref[...], sbt_ref[...]
    qt = HEAD_DIM // 4
    for h in range(GQA_HEADS):
        xt = q_in[:, h * HEAD_DIM:(h + 1) * HEAD_DIM].T
        xt = xt * lax.rsqrt(jnp.mean(xt * xt, axis=0, keepdims=True) + NORM_EPS) * qn_ref[...]
        qt_ref[h] = _rope_t(xt, cost, sat, sbt, qt).astype(jnp.bfloat16)
    for h in range(GQA_KV_HEADS):
        x = _rms(k_in[:, h * HEAD_DIM:(h + 1) * HEAD_DIM], kn_ref[...])
        k_ref[h] = _rope(x, cos, sa, sb, qt).astype(jnp.bfloat16)
        vt_ref[h] = v_in[:, h * HEAD_DIM:(h + 1) * HEAD_DIM].T.astype(jnp.bfloat16)


def gqa_prep(p, q_gain_col, k_gain, tabs, tabs_t):
    N = p.shape[0]
    tm = ROW_TILE
    wq, wk = GQA_HEADS * HEAD_DIM, GQA_KV_HEADS * HEAD_DIM
    row = lambda w, c: pl.BlockSpec((tm, w), lambda i: (i, c))
    full = lambda a: pl.BlockSpec(a.shape, lambda i: (0,) * a.ndim)
    tab = pl.BlockSpec((tm, LANES), lambda i: (i, 0))
    tab_t = pl.BlockSpec((LANES, tm), lambda i: (0, i))
    return pl.pallas_call(
        _gqa_prep_kernel,
        grid=(N // tm,),
        in_specs=[row(wq, COL_GQ // wq), row(wk, COL_GK // wk), row(wk, COL_GV // wk),
                  full(q_gain_col), full(k_gain), tab, tab, tab, tab_t, tab_t, tab_t],
        out_specs=[pl.BlockSpec((GQA_HEADS, HEAD_DIM, tm), lambda i: (0, 0, i)),
                   pl.BlockSpec((GQA_KV_HEADS, tm, HEAD_DIM), lambda i: (0, i, 0)),
                   pl.BlockSpec((GQA_KV_HEADS, HEAD_DIM, tm), lambda i: (0, 0, i))],
        out_shape=[jax.ShapeDtypeStruct((GQA_HEADS, HEAD_DIM, N), jnp.bfloat16),
                   jax.ShapeDtypeStruct((GQA_KV_HEADS, N, HEAD_DIM), jnp.bfloat16),
                   jax.ShapeDtypeStruct((GQA_KV_HEADS, HEAD_DIM, N), jnp.bfloat16)],
        compiler_params=_cparams(("arbitrary",)),
        name="gqa_prep",
    )(p, p, p, q_gain_col, k_gain, *tabs, *tabs_t)


KEY_CHUNK = 256
FLASH_UNROLL = 2


def _sublane_fold(x, op):
    return op(x.reshape(x.shape[0] // 8, 8, x.shape[1]), axis=0)


def _flash_kernel(group, q_tiles, tks, qt_ref, k_ref, vt_ref, o_ref, s_even, s_odd, m_ref, l_ref, acc_ref):
    dv = vt_ref.shape[1]
    tq = ROW_TILE
    n_chunks = tks // KEY_CHUNK
    n_super = k_ref.shape[1] // tks
    streams = [(g, t) for g in range(group) for t in range(q_tiles)]
    m_ref[...] = jnp.full(m_ref.shape, -jnp.inf, jnp.float32)
    l_ref[...] = jnp.zeros(l_ref.shape, jnp.float32)
    acc_ref[...] = jnp.zeros(acc_ref.shape, jnp.float32)

    def stage(j_a, buf_a, j_b, buf_b, cmax_b):
        if j_a is not None:
            ka = pl.multiple_of(j_a * tks, tks)
            cmax_a = [None] * len(streams)
        if j_b is not None:
            kb = pl.multiple_of(j_b * tks, tks)
            m_old = [m_ref[s] for s in range(len(streams))]
            m_new = [jnp.maximum(m_old[s], jnp.max(cmax_b[s], axis=0, keepdims=True))
                     for s in range(len(streams))]
            lsum = [jnp.zeros((8, tq), jnp.float32) for _ in streams]
            pv = [None] * len(streams)
        for c in range(n_chunks):
            rows = slice(c * KEY_CHUNK, (c + 1) * KEY_CHUNK)
            if j_a is not None:
                k_c = k_ref[0, pl.ds(ka + c * KEY_CHUNK, KEY_CHUNK), :]
            if j_b is not None:
                vt_c = vt_ref[0, :, pl.ds(kb + c * KEY_CHUNK, KEY_CHUNK)]
            for s, (g, t) in enumerate(streams):
                if j_a is not None:
                    sc = _dot(k_c, qt_ref[g, :, t * tq:(t + 1) * tq])
                    buf_a[s, rows, :] = sc
                    cm = _sublane_fold(sc, jnp.max)
                    cmax_a[s] = cm if cmax_a[s] is None else jnp.maximum(cmax_a[s], cm)
                if j_b is not None:
                    p = jnp.exp2(buf_b[s, rows, :] - m_new[s])
                    lsum[s] = lsum[s] + _sublane_fold(p, jnp.sum)
                    d = _dot(vt_c, p.astype(jnp.bfloat16))
                    pv[s] = d if pv[s] is None else d + pv[s]
        if j_b is not None:
            for s in range(len(streams)):
                alpha = jnp.exp2(m_old[s] - m_new[s])
                l_ref[s] = alpha * l_ref[s] + jnp.sum(lsum[s], axis=0, keepdims=True)
                acc_ref[s] = alpha * acc_ref[s] + pv[s]
                m_ref[s] = m_new[s]
        return tuple(cmax_a) if j_a is not None else None

    bufs = (s_even, s_odd)

    def stages(j0, count, cm):
        for d in range(count):
            cm = stage(j0 + d, bufs[(1 + d) % 2], j0 + d - 1, bufs[d % 2], cm)
        return cm

    n_loop = (n_super - 1) // FLASH_UNROLL
    cmax = stage(0, s_even, None, None, None)
    cmax = lax.fori_loop(0, n_loop, lambda i, cm: stages(FLASH_UNROLL * i + 1, FLASH_UNROLL, cm), cmax)
    cmax = stages(FLASH_UNROLL * n_loop + 1, (n_super - 1) % FLASH_UNROLL, cmax)
    stage(None, None, n_super - 1, bufs[(n_super - 1) % 2], cmax)
    for s, (g, t) in enumerate(streams):
        o_ref[t * tq:(t + 1) * tq, g * dv:(g + 1) * dv] = (acc_ref[s] / l_ref[s]).T.astype(o_ref.dtype)


def flash_attention(qt, k, vt, *, group, q_tiles, tks, q_row0, n_q, k_row0, n_k):
    hk, _, dq = k.shape
    dv = vt.shape[1]
    tq = ROW_TILE * q_tiles
    assert q_row0 % tq == 0 and n_q % tq == 0 and k_row0 % n_k == 0 and n_k % tks == 0
    assert tks % KEY_CHUNK == 0
    qb, kb = q_row0 // tq, k_row0 // n_k
    n_str = group * q_tiles
    return pl.pallas_call(
        functools.partial(_flash_kernel, group, q_tiles, tks),
        grid=(hk, n_q // tq),
        in_specs=[pl.BlockSpec((group, dq, tq), lambda h, i: (h, 0, qb + i)),
                  pl.BlockSpec((1, n_k, dq), lambda h, i: (h, kb, 0)),
                  pl.BlockSpec((1, dv, n_k), lambda h, i: (h, 0, kb))],
        out_specs=pl.BlockSpec((tq, group * dv), lambda h, i: (i, h)),
        out_shape=jax.ShapeDtypeStruct((n_q, hk * group * dv), jnp.bfloat16),
        scratch_shapes=[pltpu.VMEM((n_str, tks, ROW_TILE), jnp.float32),
                        pltpu.VMEM((n_str, tks, ROW_TILE), jnp.float32),
                        pltpu.VMEM((n_str, 1, ROW_TILE), jnp.float32),
                        pltpu.VMEM((n_str, 1, ROW_TILE), jnp.float32),
                        pltpu.VMEM((n_str, dv, ROW_TILE), jnp.float32)],
        compiler_params=_cparams(("arbitrary", "arbitrary")),
        name="flash_attention",
    )(qt, k, vt)


def _gla_block(q_ref, k_ref, v_ref, c_ref, o_ref, s_ref, ge_ref, bd_ref, r0, forward):
    nb = GLA_BLOCK
    rows = pl.ds(r0, nb)
    q, k, v, b = q_ref[rows, :], k_ref[rows, :], v_ref[rows, :], c_ref[rows, :]
    total = b[nb - 1:nb, :] if forward else b[0:1, :]
    n_pair = s_ref.shape[0]
    wk, wv = 2 * GLA_DK, 2 * GLA_DV
    qe = (q * jnp.exp(b)).astype(jnp.bfloat16)
    o = jnp.concatenate([_dot_nt(qe[:, i * wk:(i + 1) * wk], s_ref[i].astype(jnp.bfloat16))
                         for i in range(n_pair)], axis=1)
    rid = lax.broadcasted_iota(jnp.int32, b.shape, 0)
    terms = []
    for j in range(nb):
        keep = (rid >= j) if forward else (rid <= j)
        w = jnp.exp(jnp.where(keep, b - b[j:j + 1, :], -jnp.inf))
        terms.append((q * w * k[j:j + 1, :]).astype(jnp.bfloat16))
    scores = _dot(jnp.concatenate(terms, axis=0), ge_ref[...])
    for j in range(nb):
        o = o + scores[j * nb:(j + 1) * nb, :] * v[j:j + 1, :]
    o_ref[rows, :] = o
    ke = (k * jnp.exp(total - b)).astype(jnp.bfloat16)
    vb = v.astype(jnp.bfloat16)
    decay = jnp.exp(total)
    for i in range(n_pair):
        upd = lax.dot_general(vb[:, i * wv:(i + 1) * wv], ke[:, i * wk:(i + 1) * wk],
                              (((0,), (0,)), ((), ())), preferred_element_type=jnp.float32)
        s_ref[i] = decay[:, i * wk:(i + 1) * wk] * s_ref[i] + bd_ref[...] * upd


def _gla_kernel(qf, kf, vf, mf, qb, kb, vb, mb, wgf, bgf, wgb, bgb, tri_f, tri_b, ge, bd,
                of, ob, cf_ref, cb_ref, sf_ref, sb_ref):
    @pl.when(pl.program_id(0) == 0)
    def _():
        sf_ref[...] = jnp.zeros(sf_ref.shape, jnp.float32)
        sb_ref[...] = jnp.zeros(sb_ref.shape, jnp.float32)

    def cum_log_decay(m_ref, w_ref, b_ref, tri_ref):
        z = _dot_split(m_ref[...], w_ref[...]) + b_ref[...]
        la = (jnp.minimum(z, 0.0) - jnp.log1p(jnp.exp(-jnp.abs(z)))) * (1.0 / GLA_GATE_NORM)
        la_hi, la_lo = _split_bf16(la)
        tri = tri_ref[...]
        return _dot(tri, la_hi) + _dot(tri, la_lo)

    cf_ref[...] = cum_log_decay(mf, wgf, bgf, tri_f)
    cb_ref[...] = cum_log_decay(mb, wgb, bgb, tri_b)
    n_blocks = qf.shape[0] // GLA_BLOCK

    def body(t, carry):
        r_f = pl.multiple_of(t * GLA_BLOCK, GLA_BLOCK)
        r_b = pl.multiple_of((n_blocks - 1 - t) * GLA_BLOCK, GLA_BLOCK)
        _gla_block(qf, kf, vf, cf_ref, of, sf_ref, ge, bd, r_f, True)
        _gla_block(qb, kb, vb, cb_ref, ob, sb_ref, ge, bd, r_b, False)
        return carry

    lax.fori_loop(0, n_blocks, body, 0)


def gla_scan(p, w_gf_p, b_gf, w_gb_p, b_gb, consts):
    N = p.shape[0]
    tm = ROW_TILE
    nt = N // tm
    nk, nv = GLA_HEADS * GLA_DK, GLA_HEADS * GLA_DV
    fwd = lambda i: (i + nt - 1) % nt
    bwd = lambda i: (2 * nt - 1 - i) % nt
    col = lambda w, c, order: pl.BlockSpec((tm, w), lambda i: (order(i), c))
    full = lambda a: pl.BlockSpec(a.shape, lambda i: (0,) * a.ndim)
    side = lambda order: [col(nk, COL_LQ // nk, order), col(nk, COL_LK // nk, order),
                          col(nv, COL_LV // nv, order), col(LANES, COL_MISC // LANES, order)]
    small = (w_gf_p, b_gf, w_gb_p, b_gb) + tuple(consts)
    return pl.pallas_call(
        _gla_kernel,
        grid=(nt,),
        in_specs=side(fwd) + side(bwd) + [full(a) for a in small],
        out_specs=[pl.BlockSpec((tm, nv), lambda i: (fwd(i), 0)),
                   pl.BlockSpec((tm, nv), lambda i: (bwd(i), 0))],
        out_shape=[jax.ShapeDtypeStruct((N, nv), jnp.float32)] * 2,
        scratch_shapes=[pltpu.VMEM((tm, nk), jnp.float32), pltpu.VMEM((tm, nk), jnp.float32),
                        pltpu.VMEM((GLA_HEADS // 2, 2 * GLA_DV, 2 * GLA_DK), jnp.float32),
                        pltpu.VMEM((GLA_HEADS // 2, 2 * GLA_DV, 2 * GLA_DK), jnp.float32)],
        compiler_params=_cparams(("arbitrary",)),
        name="gla_scan",
    )(p, p, p, p, p, p, p, p, *small)


def _gla_consts():
    i = jnp.arange(ROW_TILE)
    same = (i[:, None] // GLA_BLOCK) == (i[None, :] // GLA_BLOCK)
    tri_f = (same & (i[None, :] <= i[:, None])).astype(jnp.bfloat16)
    tri_b = (same & (i[None, :] >= i[:, None])).astype(jnp.bfloat16)
    hk = jnp.arange(GLA_HEADS * GLA_DK) // GLA_DK
    hv = jnp.arange(GLA_HEADS * GLA_DV) // GLA_DV
    ge = (hk[:, None] == hv[None, :]).astype(jnp.bfloat16)
    bd = (hv[:2 * GLA_DV, None] == hk[None, :2 * GLA_DK]).astype(jnp.float32)
    return tri_f, tri_b, ge, bd


def _prep_gate(w_f, b_f, w_b, b_b):
    z = jnp.zeros((LANES, w_f.shape[1]), jnp.float32)
    return (z.at[MLA_ROPE:MLA_ROPE + GLA_GATE_RANK].set(w_f), b_f[None],
            z.at[MLA_ROPE + GLA_GATE_RANK:MLA_ROPE + 2 * GLA_GATE_RANK].set(w_b), b_b[None])


def _layer_norm(u, g, b):
    mu = jnp.mean(u, axis=-1, keepdims=True)
    d = u - mu
    var = jnp.mean(d * d, axis=-1, keepdims=True)
    return d * lax.rsqrt(var + LN_EPS) * g + b


def _first_max(rows):
    best, idx = rows[0], jnp.zeros(rows[0].shape, jnp.int32)
    for i in range(1, len(rows)):
        better = rows[i] > best
        idx = jnp.where(better, i, idx)
        best = jnp.where(better, rows[i], best)
    return best, idx


def _route(lt):
    grp_rows = [lt[g:g + 1, :] for g in range(N_GROUPS)]
    g_best, grp = _first_max(grp_rows)
    grp_w = 1.0 / sum(jnp.exp(r - g_best) for r in grp_rows)
    in_grp = []
    for j in range(EXPERTS_PER_GROUP):
        val = lt[8 + j:9 + j, :]
        for g in range(1, N_GROUPS):
            r = 8 + g * EXPERTS_PER_GROUP + j
            val = jnp.where(grp == g, lt[r:r + 1, :], val)
        in_grp.append(val)
    v1, i1 = _first_max(in_grp)
    rest = [jnp.where(i1 == j, -jnp.inf, in_grp[j]) for j in range(EXPERTS_PER_GROUP)]
    v2, i2 = _first_max(rest)
    e = jnp.exp(v2 - v1)
    w1 = 1.0 / (1.0 + e)
    ids = jnp.concatenate([grp * EXPERTS_PER_GROUP + i1, grp * EXPERTS_PER_GROUP + i2], axis=0)
    gates = jnp.concatenate([grp_w * w1, grp_w * (e * w1)], axis=0)
    return ids, gates


def _outproj_kernel(n_lat_tiles, alpha,
                    x_ref, yml, ymc, ygl, ygc, of_ref, ob_ref, g_ref, on_ref, wo_ref,
                    g1_ref, sh2_ref, sc2_ref, lg_ref, lb_ref, wr_ref, br_ref,
                    x1_ref, h2_ref, id_ref, gt_ref):
    is_ctx = pl.program_id(0) >= n_lat_tiles
    y_mla = jnp.where(is_ctx, ymc[...], yml[...])
    y_gqa = jnp.where(is_ctx, ygc[...], ygl[...])
    o = of_ref[...] + ob_ref[...]
    heads = [_rms(o[:, h * GLA_DV:(h + 1) * GLA_DV], on_ref[...]) for h in range(GLA_HEADS)]
    g = g_ref[...]
    y_gla = (jnp.concatenate(heads, axis=1) * (g * jax.nn.sigmoid(g))).astype(jnp.bfloat16)
    w_mla, w_gqa = y_mla.shape[1], y_gqa.shape[1]
    y = (_dot(y_mla, wo_ref[0:w_mla, :]) + _dot(y_gqa, wo_ref[w_mla:w_mla + w_gqa, :])
         + _dot(y_gla, wo_ref[w_mla + w_gqa:, :]))
    x1 = _layer_norm(alpha * x_ref[...] + _mod_row(g1_ref, is_ctx) * y, lg_ref[...], lb_ref[...])
    x1_ref[...] = x1
    h2 = x1 * (1.0 + _mod_row(sc2_ref, is_ctx)) + _mod_row(sh2_ref, is_ctx)
    h2_ref[...] = h2
    lt = _dot_split(wr_ref[...], h2, _dot_nt) + br_ref[...]
    ids, gates = _route(lt)
    rid = lax.broadcasted_iota(jnp.int32, id_ref.shape, 0)
    id_ref[...] = jnp.where(rid == 0, ids[0:1], jnp.where(rid == 1, ids[1:2], 0))
    gt_ref[...] = jnp.where(rid == 0, gates[0:1], jnp.where(rid == 1, gates[1:2], 0.0))


def out_projection(xs, y_mla, y_gqa, o_f, o_b, p, out_norm, w_out, mod_l, ln_g, ln_b, w_rt, b_rt,
                   n_lat, alpha):
    N, D = xs.shape
    tm = ROW_TILE
    nl = n_lat // tm
    assert N - n_lat == tm
    nv = GLA_HEADS * GLA_DV
    lat = lambda a: pl.BlockSpec((tm, a.shape[1]), lambda i: (jnp.minimum(i, nl - 1), 0))
    ctx = lambda a: pl.BlockSpec((tm, a.shape[1]), lambda i: (0, 0))
    row = lambda w, c: pl.BlockSpec((tm, w), lambda i: (i, c))
    full = lambda a: pl.BlockSpec(a.shape, lambda i: (0,) * a.ndim)
    modc = lambda c: pl.BlockSpec((8, D), lambda i: (0, c))
    (yml, ymc), (ygl, ygc) = y_mla, y_gqa
    return pl.pallas_call(
        functools.partial(_outproj_kernel, nl, alpha),
        grid=(N // tm,),
        in_specs=[row(D, 0), lat(yml), ctx(ymc), lat(ygl), ctx(ygc), row(nv, 0), row(nv, 0),
                  row(nv, COL_LG // nv), full(out_norm), full(w_out),
                  modc(2), modc(3), modc(4), full(ln_g), full(ln_b), full(w_rt), full(b_rt)],
        out_specs=[row(D, 0), row(D, 0), pl.BlockSpec((8, tm), lambda i: (0, i)),
                   pl.BlockSpec((8, tm), lambda i: (0, i))],
        out_shape=[jax.ShapeDtypeStruct((N, D), jnp.float32), jax.ShapeDtypeStruct((N, D), jnp.float32),
                   jax.ShapeDtypeStruct((8, N), jnp.int32), jax.ShapeDtypeStruct((8, N), jnp.float32)],
        compiler_params=_cparams(("arbitrary",)),
        name="out_projection",
    )(xs, yml, ymc, ygl, ygc, o_f, o_b, p, out_norm, w_out, mod_l, mod_l, mod_l, ln_g, ln_b, w_rt, b_rt)


def _row_copy(src_hbm, idx_ref, dst, sem, r):
    return pltpu.make_async_copy(src_hbm.at[pl.ds(idx_ref[0, 0, r], 1), :], dst.at[pl.ds(r, 1), :], sem)


def _start_rows(src_hbm, idx_ref, dst, sem, unrolled):
    n = dst.shape[0]
    if unrolled:
        for r in range(n):
            _row_copy(src_hbm, idx_ref, dst, sem, r).start()
    else:
        def start(r, c):
            _row_copy(src_hbm, idx_ref, dst, sem, r).start()
            return c
        lax.fori_loop(0, n, start, 0, unroll=8)


def _wait_rows(src_hbm, dst, sem):
    pltpu.make_async_copy(src_hbm.at[pl.ds(0, dst.shape[0]), :], dst, sem).wait()


def _double_buffered_gather(i, n_steps, src_hbm, idx_ref, idx_next_ref, bufs, sems, compute):
    @pl.when(i == 0)
    def _():
        _start_rows(src_hbm, idx_ref, bufs[0], sems.at[0], unrolled=False)

    for parity in (0, 1):
        cur, nxt = bufs[parity], bufs[1 - parity]
        for has_next in (True, False):
            more = i + 1 < n_steps
            cond = (i < n_steps) & (i % 2 == parity) & (more if has_next else jnp.logical_not(more))

            @pl.when(cond)
            def _(cur=cur, nxt=nxt, parity=parity, has_next=has_next):
                _wait_rows(src_hbm, cur, sems.at[parity])
                if has_next:
                    _start_rows(src_hbm, idx_next_ref, nxt, sems.at[1 - parity], unrolled=True)
                compute(cur)


def _expert_kernel(te_ref, nu_ref, idx_ref, idx_next_ref, h_hbm, wg_ref, wu_ref, wd_ref, y_ref,
                   xa, xb, wg_bf, wu_bf, wd_bf, sems):
    i = pl.program_id(0)
    n_used = nu_ref[0]

    @pl.when((i < n_used) & ((i == 0) | (te_ref[i] != te_ref[jnp.maximum(i - 1, 0)])))
    def _():
        wg_bf[...] = wg_ref[0, 0].astype(jnp.bfloat16)
        wu_bf[...] = wu_ref[0, 0].astype(jnp.bfloat16)
        wd_bf[...] = wd_ref[0, 0].astype(jnp.bfloat16)

    def compute(x_ref):
        x = x_ref[...].astype(jnp.bfloat16)
        a = _dot(x, wg_bf[...])
        u = _dot(x, wu_bf[...])
        hid = (a * jax.nn.sigmoid(a) * u).astype(jnp.bfloat16)
        y_ref[...] = _dot(hid, wd_bf[...])

    _double_buffered_gather(i, n_used, h_hbm, idx_ref, idx_next_ref, (xa, xb), sems, compute)

    @pl.when(i >= n_used)
    def _():
        y_ref[...] = jnp.zeros(y_ref.shape, jnp.float32)


def expert_ffn(h2, tile_expert, n_used, sorted_tok, w_g, w_u, w_d, layer):
    N, D = h2.shape
    tm = MOE_TILE
    n_tiles = sorted_tok.shape[0] // tm
    F = w_g.shape[3]
    bf = jnp.bfloat16
    idx = sorted_tok.reshape(n_tiles, 1, tm)
    nxt = lambda i, te, nu: (jnp.minimum(i + 1, n_tiles - 1), 0, 0)
    gs = pltpu.PrefetchScalarGridSpec(
        num_scalar_prefetch=2,
        grid=(n_tiles,),
        in_specs=[pl.BlockSpec((1, 1, tm), lambda i, te, nu: (i, 0, 0), memory_space=pltpu.SMEM),
                  pl.BlockSpec((1, 1, tm), nxt, memory_space=pltpu.SMEM),
                  pl.BlockSpec(memory_space=pl.ANY),
                  pl.BlockSpec((1, 1, D, F), lambda i, te, nu: (layer, te[i], 0, 0)),
                  pl.BlockSpec((1, 1, D, F), lambda i, te, nu: (layer, te[i], 0, 0)),
                  pl.BlockSpec((1, 1, F, D), lambda i, te, nu: (layer, te[i], 0, 0))],
        out_specs=pl.BlockSpec((tm, D), lambda i, te, nu: (i, 0)),
        scratch_shapes=[pltpu.VMEM((tm, D), jnp.float32), pltpu.VMEM((tm, D), jnp.float32),
                        pltpu.VMEM((D, F), bf), pltpu.VMEM((D, F), bf), pltpu.VMEM((F, D), bf),
                        pltpu.SemaphoreType.DMA((2,))],
    )
    return pl.pallas_call(
        _expert_kernel,
        grid_spec=gs,
        out_shape=jax.ShapeDtypeStruct((n_tiles * tm, D), jnp.float32),
        compiler_params=_cparams(("arbitrary",)),
        name="expert_ffn",
    )(tile_expert, n_used, idx, idx, h2, w_g, w_u, w_d)


def _combine_kernel(n_lat_tiles, alpha, pos_ref, pos_next_ref, y_hbm, x_ref, gc_ref, g2_ref, lg_ref,
                    lb_ref, o_ref, ya, yb, sems):
    i = pl.program_id(0)
    is_ctx = i >= n_lat_tiles
    tm = x_ref.shape[0]

    def compute(y_ref):
        gc = gc_ref[...]
        f = gc[:, 0:1] * y_ref[0:tm, :] + gc[:, 1:2] * y_ref[tm:2 * tm, :]
        o_ref[...] = _layer_norm(alpha * x_ref[...] + _mod_row(g2_ref, is_ctx) * f, lg_ref[...], lb_ref[...])

    _double_buffered_gather(i, pl.num_programs(0), y_hbm, pos_ref, pos_next_ref, (ya, yb), sems, compute)


def moe_combine(xs1, y_sorted, pos, gate_cols, mod_l, ln_g, ln_b, n_lat, alpha):
    N, D = xs1.shape
    tm = ROW_TILE
    nt = N // tm
    full = lambda a: pl.BlockSpec(a.shape, lambda i: (0,) * a.ndim)
    return pl.pallas_call(
        functools.partial(_combine_kernel, n_lat // tm, alpha),
        grid=(nt,),
        in_specs=[pl.BlockSpec((1, 1, 2 * tm), lambda i: (i, 0, 0), memory_space=pltpu.SMEM),
                  pl.BlockSpec((1, 1, 2 * tm), lambda i: (jnp.minimum(i + 1, nt - 1), 0, 0),
                               memory_space=pltpu.SMEM),
                  pl.BlockSpec(memory_space=pl.ANY),
                  pl.BlockSpec((tm, D), lambda i: (i, 0)),
                  pl.BlockSpec((tm, 2), lambda i: (i, 0)),
                  pl.BlockSpec((8, D), lambda i: (0, 5)), full(ln_g), full(ln_b)],
        out_specs=pl.BlockSpec((tm, D), lambda i: (i, 0)),
        out_shape=jax.ShapeDtypeStruct((N, D), jnp.float32),
        scratch_shapes=[pltpu.VMEM((2 * tm, D), jnp.float32), pltpu.VMEM((2 * tm, D), jnp.float32),
                        pltpu.SemaphoreType.DMA((2,))],
        compiler_params=_cparams(("arbitrary",)),
        name="moe_combine",
    )(pos, pos, y_sorted, xs1, gate_cols, mod_l, ln_g, ln_b)


def _dispatch_plan(ids, n_tok):
    tm = MOE_TILE
    n_asg = 2 * n_tok
    assert n_asg * n_asg < 2 ** 31
    n_tiles = n_asg // tm + N_EXPERTS
    i32 = jnp.int32
    e_flat = ids.reshape(n_asg)
    ar = jnp.arange(n_asg, dtype=i32)
    order = jnp.sort(e_flat * n_asg + ar) % n_asg
    rank = jnp.sort(order * n_asg + ar) % n_asg
    experts = jnp.arange(N_EXPERTS, dtype=i32)
    onehot = e_flat[:, None] == experts[None, :]
    counts = jnp.sum(onehot, axis=0, dtype=i32)
    ends = jnp.cumsum(counts).astype(i32)
    padded = ((counts + tm - 1) // tm) * tm
    pad_ends = jnp.cumsum(padded).astype(i32)
    shift = (pad_ends - padded) - (ends - counts)
    pos = rank + jnp.sum(jnp.where(onehot, shift[None, :], 0), axis=1, dtype=i32)
    n_used = (pad_ends[-1] // tm).reshape(1)
    tile_start = jnp.arange(n_tiles, dtype=i32) * tm
    tile_expert = jnp.sum(tile_start[:, None] >= pad_ends[None, :], axis=1, dtype=i32)
    last = jnp.sum((pad_ends[-1] - 1) >= pad_ends, dtype=i32)
    tile_expert = jnp.minimum(tile_expert, last)
    of_tile = tile_expert[:, None] == experts[None, :]
    shift_t = jnp.sum(jnp.where(of_tile, shift[None, :], 0), axis=1, dtype=i32)
    ends_t = jnp.sum(jnp.where(of_tile, ends[None, :], 0), axis=1, dtype=i32)
    r = tile_start[:, None] + jnp.arange(tm, dtype=i32)[None, :] - shift_t[:, None]
    valid = (r < ends_t[:, None]) & (tile_start[:, None] < pad_ends[-1])
    sorted_tok = jnp.where(valid, order[jnp.clip(r, 0, n_asg - 1)] % n_tok, 0).reshape(n_tiles * tm)
    nt = n_tok // ROW_TILE
    pos_tiles = jnp.concatenate([pos[:n_tok].reshape(nt, 1, ROW_TILE),
                                 pos[n_tok:].reshape(nt, 1, ROW_TILE)], axis=2)
    return sorted_tok, tile_expert, n_used, pos_tiles


def _rope_tables(n_lat, n_ctx, dim):
    rows = n_lat // GRID_W
    row = jnp.repeat(jnp.arange(rows), GRID_W).astype(jnp.float32)
    col = jnp.tile(jnp.arange(GRID_W), rows).astype(jnp.float32)
    half = dim // 2
    inv_freq = ROPE_THETA ** (-jnp.arange(0, half, 2, dtype=jnp.float32) / half)
    ang_r = row[:, None] * inv_freq
    ang_c = col[:, None] * inv_freq
    ang = jnp.concatenate([ang_r, ang_r, ang_c, ang_c], axis=-1)
    cos, sin = jnp.cos(ang), jnp.sin(ang)
    q = dim // 4
    lane = jnp.arange(dim)
    first = ((lane // q) % 2) == 0
    sin_a = jnp.where(first, -sin, 0.0)
    sin_b = jnp.where(first, 0.0, sin)

    def pad(t, fill):
        t = jnp.pad(t, ((0, 0), (0, LANES - dim)), constant_values=fill)
        return jnp.pad(t, ((0, n_ctx), (0, 0)), constant_values=fill)

    return pad(cos, 1.0), pad(sin_a, 0.0), pad(sin_b, 0.0)


def _in_perm():
    a = MLA_Q_RANK + MLA_KV_RANK + MLA_ROPE
    b = a + (GQA_HEADS + 2 * GQA_KV_HEADS) * HEAD_DIM
    nk, nv = GLA_HEADS * GLA_DK, GLA_HEADS * GLA_DV
    r = lambda s, n: list(range(s, s + n))
    cols = (r(0, MLA_Q_RANK + MLA_KV_RANK)
            + r(a, b - a)
            + r(b, 2 * nk + 2 * nv)
            + r(MLA_Q_RANK + MLA_KV_RANK, MLA_ROPE)
            + r(b + 2 * nk + 2 * nv, 2 * GLA_GATE_RANK))
    return jnp.asarray(cols, jnp.int32)


def _prep_w_in(w_in_l):
    w = w_in_l[:, _in_perm()]
    scale = jnp.ones((w.shape[1],), jnp.float32).at[COL_LQ:COL_LK].set(GLA_DK ** -0.5)
    w = w * scale
    return jnp.pad(w, ((0, 0), (0, P_COLS - w.shape[1]))).astype(jnp.bfloat16)


def _prep_w_uq(w_uq_l):
    r = w_uq_l.shape[0]
    w = w_uq_l.reshape(r, MLA_HEADS, MLA_NOPE + MLA_ROPE)
    w = jnp.pad(w, ((0, 0), (0, 0), (0, MLA_QPAD - MLA_NOPE - MLA_ROPE)))
    return w.reshape(r, MLA_HEADS * MLA_QPAD).astype(jnp.bfloat16)


def _prep_router(w_rg, b_rg, w_re, b_re):
    D = w_rg.shape[0]
    w = jnp.zeros((LANES, D), jnp.float32).at[0:N_GROUPS].set(w_rg.T).at[8:8 + N_EXPERTS].set(w_re.T)
    b = jnp.zeros((LANES,), jnp.float32).at[0:N_GROUPS].set(b_rg).at[8:8 + N_EXPERTS].set(b_re)
    return w, b[:, None]


def _kv_tile(n):
    return max(t for t in (256, 640, 1280) if n % t == 0)


def kernel(x, c, ctx, c_ctx, ada_w, ada_b, w_in, mla_q_norm, mla_w_uq, mla_kv_norm, mla_w_ukv, gqa_q_norm, gqa_k_norm, gla_w_gate_fwd, gla_b_gate_fwd, gla_w_gate_bwd, gla_b_gate_bwd, gla_out_norm, w_out, ln1_g, ln1_b, w_route_group, b_route_group, w_route_expert, b_route_expert, w_expert_gate, w_expert_up, w_expert_down, ln2_g, ln2_b):
    T, C = x.shape[1], ctx.shape[1]
    N = T + C
    L = ada_w.shape[0]
    assert x.shape[0] == 1 and C == ROW_TILE and T % ROW_TILE == 0 and T % GRID_W == 0
    alpha = (2.0 * L) ** 0.25
    bf = jnp.bfloat16

    xs = jnp.concatenate([x[0], ctx[0]], axis=0)
    mods = modulation(c, c_ctx, ada_w, ada_b)
    tabs_mla = _rope_tables(T, C, MLA_ROPE)
    tabs_gqa = _rope_tables(T, C, HEAD_DIM)
    gla_consts = _gla_consts()
    mla_scale = (MLA_NOPE + MLA_ROPE) ** -0.5 * LOG2E
    gqa_scale = HEAD_DIM ** -0.5 * LOG2E
    tabs_mla_t = tuple(t.T for t in tabs_mla)
    tabs_gqa_t = tuple(t.T for t in tabs_gqa)
    lat = dict(q_row0=0, n_q=T, k_row0=0, n_k=N, tks=_kv_tile(N))
    con = dict(q_row0=T, n_q=C, k_row0=T, n_k=C, tks=C)

    for l in range(L):
        mod_l = mods[l]
        p = in_projection(xs, mod_l, _prep_w_in(w_in[l]), T)

        w_ukv = mla_w_ukv[l].reshape(MLA_KV_RANK, MLA_HEADS, MLA_NOPE + MLA_V)
        w_kn = w_ukv[:, :, :MLA_NOPE].reshape(MLA_KV_RANK, MLA_HEADS * MLA_NOPE).astype(bf)
        w_v_t = w_ukv[:, :, MLA_NOPE:].reshape(MLA_KV_RANK, MLA_HEADS * MLA_V).T.astype(bf)
        qt, k, vt = mla_prep(p, (mla_q_norm[l] * mla_scale)[None], _prep_w_uq(mla_w_uq[l]).T,
                             mla_kv_norm[l][None], w_kn, w_v_t, tabs_mla, tabs_mla_t)
        y_mla = (flash_attention(qt, k, vt, group=1, q_tiles=2 if T % (2 * ROW_TILE) == 0 else 1, **lat),
                 flash_attention(qt, k, vt, group=1, q_tiles=1, **con))
        qt, k, vt = gqa_prep(p, (gqa_q_norm[l] * gqa_scale)[:, None], gqa_k_norm[l][None],
                             tabs_gqa, tabs_gqa_t)
        y_gqa = (flash_attention(qt, k, vt, group=GQA_GROUP, q_tiles=1, **lat),
                 flash_attention(qt, k, vt, group=GQA_GROUP, q_tiles=1, **con))
        o_f, o_b = gla_scan(p, *_prep_gate(gla_w_gate_fwd[l], gla_b_gate_fwd[l],
                                           gla_w_gate_bwd[l], gla_b_gate_bwd[l]), gla_consts)

        w_rt, b_rt = _prep_router(w_route_group[l], b_route_group[l], w_route_expert[l], b_route_expert[l])
        x1, h2, ids, gates = out_projection(xs, y_mla, y_gqa, o_f, o_b, p, gla_out_norm[l][None],
                                            w_out[l].astype(bf), mod_l, ln1_g[l][None], ln1_b[l][None],
                                            w_rt, b_rt, T, alpha)

        sorted_tok, tile_expert, n_used, pos_tiles = _dispatch_plan(ids[0:2], N)
        y_sorted = expert_ffn(h2, tile_expert, n_used, sorted_tok, w_expert_gate, w_expert_up,
                              w_expert_down, l)
        xs = moe_combine(x1, y_sorted, pos_tiles, gates[0:2].T, mod_l, ln2_g[l][None], ln2_b[l][None],
                         T, alpha)
    return xs[:T][None]
```

```python
import functools
import math

import jax
import jax.numpy as jnp
from jax import lax
from jax.experimental import pallas as pl
from jax.experimental.pallas import tpu as pltpu

GRID_W = 64
ROPE_THETA = 10000.0
NORM_EPS = 1e-6
LN_EPS = 1e-5
HEAD_DIM = 128
MLA_HEADS = 6
MLA_Q_RANK = 512
MLA_KV_RANK = 256
MLA_NOPE = 128
MLA_ROPE = 64
MLA_V = 128
GQA_HEADS = 6
GQA_KV_HEADS = 2
GQA_GROUP = GQA_HEADS // GQA_KV_HEADS
GLA_HEADS = 4
GLA_DK = 64
GLA_DV = 128
GLA_GATE_RANK = 16
GLA_GATE_NORM = 16.0
N_GROUPS = 4
EXPERTS_PER_GROUP = 8
N_EXPERTS = N_GROUPS * EXPERTS_PER_GROUP
N_MOD = 6

LANES = 128
VMEM_LIMIT = 48 * 1024 * 1024

ROW_TILE = 256
GLA_BLOCK = 16
MOE_TILE = 256

COL_CQ = 0
COL_CKV = 512
COL_GQ = 768
COL_GK = 1536
COL_GV = 1792
COL_LQ = 2048
COL_LK = 2304
COL_LV = 2560
COL_LG = 3072
COL_MISC = 3584
P_COLS = 3840
P_HALF = P_COLS // 2
MLA_QPAD = 256

LOG2E = math.log2(math.e)


def _cparams(sem):
    return pltpu.CompilerParams(dimension_semantics=sem, vmem_limit_bytes=VMEM_LIMIT)


def _dot(a, b):
    return jnp.dot(a, b, preferred_element_type=jnp.float32)


def _dot_nt(a, b):
    return lax.dot_general(a, b, (((1,), (1,)), ((), ())), preferred_element_type=jnp.float32)


def _split_bf16(x):
    hi = x.astype(jnp.bfloat16)
    return hi, (x - hi.astype(jnp.float32)).astype(jnp.bfloat16)


def _dot_split(a, b, dot=_dot):
    a_hi, a_lo = _split_bf16(a)
    b_hi, b_lo = _split_bf16(b)
    return dot(a_hi, b_hi) + dot(a_lo, b_hi) + dot(a_hi, b_lo)


def _mod_kernel(crep_ref, w_ref, b_ref, o_ref):
    tn = w_ref.shape[2]
    rows = []
    for r in range(2):
        cr = crep_ref[r]
        a = cr * jax.nn.sigmoid(cr)
        parts = []
        for j in range(tn // LANES):
            wj = w_ref[0, :, j * LANES:(j + 1) * LANES]
            parts.append(jnp.sum(wj * a, axis=0, keepdims=True))
        rows.append(jnp.concatenate(parts, axis=1) + b_ref[0])
    rid = lax.broadcasted_iota(jnp.int32, (8, tn), 0)
    o_ref[0] = jnp.where(rid == 0, rows[0], jnp.where(rid == 1, rows[1], 0.0))


def modulation(c, c_ctx, ada_w, ada_b):
    L, D, M = ada_w.shape
    tn = max(t for t in range(LANES, 1024 + 1, LANES) if M % t == 0)
    crep = jnp.broadcast_to(jnp.stack([c[0], c_ctx])[:, :, None], (2, D, LANES))
    return pl.pallas_call(
        _mod_kernel,
        grid=(L, M // tn),
        in_specs=[pl.BlockSpec((2, D, LANES), lambda l, j: (0, 0, 0)),
                  pl.BlockSpec((1, D, tn), lambda l, j: (l, 0, j)),
                  pl.BlockSpec((1, 1, tn), lambda l, j: (l, 0, j))],
        out_specs=pl.BlockSpec((1, 8, tn), lambda l, j: (l, 0, j)),
        out_shape=jax.ShapeDtypeStruct((L, 8, M), jnp.float32),
        compiler_params=_cparams(("arbitrary", "arbitrary")),
        name="modulation",
    )(crep, ada_w, ada_b.reshape(L, 1, M))


def _mod_row(ref, is_ctx):
    return jnp.where(is_ctx, ref[1:2, :], ref[0:1, :])


def _inproj_kernel(n_lat_tiles, x_ref, sh_ref, sc_ref, w_ref, o_ref):
    is_ctx = pl.program_id(1) >= n_lat_tiles
    h = x_ref[...] * (1.0 + _mod_row(sc_ref, is_ctx)) + _mod_row(sh_ref, is_ctx)
    o_ref[...] = _dot(h.astype(jnp.bfloat16), w_ref[...])


def in_projection(xs, mod_l, w_in_p, n_lat):
    N, D = xs.shape
    tm = ROW_TILE
    return pl.pallas_call(
        functools.partial(_inproj_kernel, n_lat // tm),
        grid=(2, N // tm),
        in_specs=[pl.BlockSpec((tm, D), lambda j, i: (i, 0)),
                  pl.BlockSpec((8, D), lambda j, i: (0, 0)),
                  pl.BlockSpec((8, D), lambda j, i: (0, 1)),
                  pl.BlockSpec((D, P_HALF), lambda j, i: (0, j))],
        out_specs=pl.BlockSpec((tm, P_HALF), lambda j, i: (i, j)),
        out_shape=jax.ShapeDtypeStruct((N, P_COLS), jnp.float32),
        compiler_params=_cparams(("arbitrary", "arbitrary")),
        name="in_projection",
    )(xs, mod_l, mod_l, w_in_p)


def _rope(x, cos, sin_a, sin_b, quarter):
    w = x.shape[-1]
    return x * cos + pltpu.roll(x, w - quarter, 1) * sin_a + pltpu.roll(x, quarter, 1) * sin_b


def _rms(x, gain):
    return x * lax.rsqrt(jnp.mean(x * x, axis=-1, keepdims=True) + NORM_EPS) * gain


def _rope_t(x, cos, sin_a, sin_b, quarter):
    w = x.shape[0]
    return x * cos + pltpu.roll(x, w - quarter, 0) * sin_a + pltpu.roll(x, quarter, 0) * sin_b


def _mla_prep_kernel(cq_ref, ckv_ref, misc_ref, qn_ref, wuqt_ref, kvn_ref, wkn_ref, wvt_ref,
                     cos_ref, sa_ref, sb_ref, cost_ref, sat_ref, sbt_ref, qt_ref, k_ref, vt_ref):
    cq = _rms(cq_ref[...], qn_ref[...]).astype(jnp.bfloat16)
    qt_all = _dot_nt(wuqt_ref[...], cq)
    ckv = _rms(ckv_ref[...], kvn_ref[...]).astype(jnp.bfloat16)
    kn_all = _dot(ckv, wkn_ref[...])
    vt_all = _dot_nt(wvt_ref[...], ckv)
    lane = lax.broadcasted_iota(jnp.int32, misc_ref.shape, 1)
    k_rope = _rope(misc_ref[...], cos_ref[...], sa_ref[...], sb_ref[...], MLA_ROPE // 4)
    k_rope = jnp.where(lane < MLA_ROPE, k_rope, 0.0).astype(jnp.bfloat16)
    cost, sat, sbt = cost_ref[...], sat_ref[...], sbt_ref[...]
    for h in range(MLA_HEADS):
        b0 = h * MLA_QPAD
        qt_ref[h, 0:LANES, :] = qt_all[b0:b0 + LANES, :].astype(jnp.bfloat16)
        qt_ref[h, LANES:2 * LANES, :] = _rope_t(qt_all[b0 + LANES:b0 + 2 * LANES, :], cost, sat, sbt,
                                                MLA_ROPE // 4).astype(jnp.bfloat16)
        k_ref[h, :, 0:LANES] = kn_all[:, h * MLA_NOPE:(h + 1) * MLA_NOPE].astype(jnp.bfloat16)
        k_ref[h, :, LANES:2 * LANES] = k_rope
        vt_ref[h] = vt_all[h * MLA_V:(h + 1) * MLA_V, :].astype(jnp.bfloat16)


def mla_prep(p, q_gain, w_uq_t, kv_gain, w_kn, w_v_t, tabs, tabs_t):
    N = p.shape[0]
    tm = ROW_TILE
    H = MLA_HEADS
    row = lambda w, c: pl.BlockSpec((tm, w), lambda i: (i, c))
    full = lambda a: pl.BlockSpec(a.shape, lambda i: (0,) * a.ndim)
    tab = pl.BlockSpec((tm, LANES), lambda i: (i, 0))
    tab_t = pl.BlockSpec((LANES, tm), lambda i: (0, i))
    return pl.pallas_call(
        _mla_prep_kernel,
        grid=(N // tm,),
        in_specs=[row(MLA_Q_RANK, COL_CQ // MLA_Q_RANK), row(MLA_KV_RANK, COL_CKV // MLA_KV_RANK),
                  row(LANES, COL_MISC // LANES),
                  full(q_gain), full(w_uq_t), full(kv_gain), full(w_kn), full(w_v_t),
                  tab, tab, tab, tab_t, tab_t, tab_t],
        out_specs=[pl.BlockSpec((H, MLA_QPAD, tm), lambda i: (0, 0, i)),
                   pl.BlockSpec((H, tm, MLA_QPAD), lambda i: (0, i, 0)),
                   pl.BlockSpec((H, MLA_V, tm), lambda i: (0, 0, i))],
        out_shape=[jax.ShapeDtypeStruct((H, MLA_QPAD, N), jnp.bfloat16),
                   jax.ShapeDtypeStruct((H, N, MLA_QPAD), jnp.bfloat16),
                   jax.ShapeDtypeStruct((H, MLA_V, N), jnp.bfloat16)],
        compiler_params=_cparams(("arbitrary",)),
        name="mla_prep",
    )(p, p, p, q_gain, w_uq_t, kv_gain, w_kn, w_v_t, *tabs, *tabs_t)


def _gqa_prep_kernel(q_in, k_in, v_in, qn_ref, kn_ref, cos_ref, sa_ref, sb_ref,
                     cost_ref, sat_ref, sbt_ref, qt_ref, k_ref, vt_ref):
    cos, sa, sb = cos_ref[...], sa_ref[...], sb_ref[...]
    cost, sat, sbt = cost_ref[...], sat_ref[...], sbt_ref[...]
    qt = HEAD_DIM // 4
    for h in range(GQA_HEADS):
        xt = q_in[:, h * HEAD_DIM:(h + 1) * HEAD_DIM].T
        xt = xt * lax.rsqrt(jnp.mean(xt * xt, axis=0, keepdims=True) + NORM_EPS) * qn_ref[...]
        qt_ref[h] = _rope_t(xt, cost, sat, sbt, qt).astype(jnp.bfloat16)
    for h in range(GQA_KV_HEADS):
        x = _rms(k_in[:, h * HEAD_DIM:(h + 1) * HEAD_DIM], kn_ref[...])
        k_ref[h] = _rope(x, cos, sa, sb, qt).astype(jnp.bfloat16)
        vt_ref[h] = v_in[:, h * HEAD_DIM:(h + 1) * HEAD_DIM].T.astype(jnp.bfloat16)


def gqa_prep(p, q_gain_col, k_gain, tabs, tabs_t):
    N = p.shape[0]
    tm = ROW_TILE
    wq, wk = GQA_HEADS * HEAD_DIM, GQA_KV_HEADS * HEAD_DIM
    row = lambda w, c: pl.BlockSpec((tm, w), lambda i: (i, c))
    full = lambda a: pl.BlockSpec(a.shape, lambda i: (0,) * a.ndim)
    tab = pl.BlockSpec((tm, LANES), lambda i: (i, 0))
    tab_t = pl.BlockSpec((LANES, tm), lambda i: (0, i))
    return pl.pallas_call(
        _gqa_prep_kernel,
        grid=(N // tm,),
        in_specs=[row(wq, COL_GQ // wq), row(wk, COL_GK // wk), row(wk, COL_GV // wk),
                  full(q_gain_col), full(k_gain), tab, tab, tab, tab_t, tab_t, tab_t],
        out_specs=[pl.BlockSpec((GQA_HEADS, HEAD_DIM, tm), lambda i: (0, 0, i)),
                   pl.BlockSpec((GQA_KV_HEADS, tm, HEAD_DIM), lambda i: (0, i, 0)),
                   pl.BlockSpec((GQA_KV_HEADS, HEAD_DIM, tm), lambda i: (0, 0, i))],
        out_shape=[jax.ShapeDtypeStruct((GQA_HEADS, HEAD_DIM, N), jnp.bfloat16),
                   jax.ShapeDtypeStruct((GQA_KV_HEADS, N, HEAD_DIM), jnp.bfloat16),
                   jax.ShapeDtypeStruct((GQA_KV_HEADS, HEAD_DIM, N), jnp.bfloat16)],
        compiler_params=_cparams(("arbitrary",)),
        name="gqa_prep",
    )(p, p, p, q_gain_col, k_gain, *tabs, *tabs_t)


KEY_CHUNK = 256
FLASH_UNROLL = 2


def _sublane_fold(x, op):
    return op(x.reshape(x.shape[0] // 8, 8, x.shape[1]), axis=0)


def _flash_kernel(group, q_tiles, q_groups, tks, qt_ref, k_ref, vt_ref, o_ref, s_even, s_odd,
                  m_ref, l_ref, acc_ref):
    dv = vt_ref.shape[1]
    tq = ROW_TILE
    n_chunks = tks // KEY_CHUNK
    n_super = k_ref.shape[1] // tks
    streams = [(g, t) for g in range(group) for t in range(q_tiles)]

    def reset_state():
        m_ref[...] = jnp.full(m_ref.shape, -jnp.inf, jnp.float32)
        l_ref[...] = jnp.zeros(l_ref.shape, jnp.float32)
        acc_ref[...] = jnp.zeros(acc_ref.shape, jnp.float32)

    def stage(qg_a, j_a, buf_a, j_b, buf_b, cmax_b):
        if j_a is not None:
            col0 = qg_a * q_tiles * tq
            ka = pl.multiple_of(j_a * tks, tks)
            cmax_a = [None] * len(streams)
        if j_b is not None:
            kb = pl.multiple_of(j_b * tks, tks)
            m_old = [m_ref[s] for s in range(len(streams))]
            m_new = [jnp.maximum(m_old[s], jnp.max(cmax_b[s], axis=0, keepdims=True))
                     for s in range(len(streams))]
            lsum = [jnp.zeros((8, tq), jnp.float32) for _ in streams]
            pv = [None] * len(streams)
        for c in range(n_chunks):
            rows = slice(c * KEY_CHUNK, (c + 1) * KEY_CHUNK)
            if j_a is not None:
                k_c = k_ref[0, pl.ds(ka + c * KEY_CHUNK, KEY_CHUNK), :]
            if j_b is not None:
                vt_c = vt_ref[0, :, pl.ds(kb + c * KEY_CHUNK, KEY_CHUNK)]
            for s, (g, t) in enumerate(streams):
                if j_a is not None:
                    sc = _dot(k_c, qt_ref[g, :, col0 + t * tq:col0 + (t + 1) * tq])
                    buf_a[s, rows, :] = sc
                    cm = _sublane_fold(sc, jnp.max)
                    cmax_a[s] = cm if cmax_a[s] is None else jnp.maximum(cmax_a[s], cm)
                if j_b is not None:
                    p = jnp.exp2(buf_b[s, rows, :] - m_new[s])
                    lsum[s] = lsum[s] + _sublane_fold(p, jnp.sum)
                    d = _dot(vt_c, p.astype(jnp.bfloat16))
                    pv[s] = d if pv[s] is None else d + pv[s]
        if j_b is not None:
            for s in range(len(streams)):
                alpha = jnp.exp2(m_old[s] - m_new[s])
                l_ref[s] = alpha * l_ref[s] + jnp.sum(lsum[s], axis=0, keepdims=True)
                acc_ref[s] = alpha * acc_ref[s] + pv[s]
                m_ref[s] = m_new[s]
        return tuple(cmax_a) if j_a is not None else None

    bufs = (s_even, s_odd)

    def stages(qg, j0, count, cm):
        base = qg * n_super
        for d in range(count):
            cm = stage(qg, j0 + d, bufs[(base + 1 + d) % 2], j0 + d - 1, bufs[(base + d) % 2], cm)
        return cm

    def finish(qg):
        row0 = qg * q_tiles * tq
        for s, (g, t) in enumerate(streams):
            o_ref[row0 + t * tq:row0 + (t + 1) * tq, g * dv:(g + 1) * dv] = (
                acc_ref[s] / l_ref[s]).T.astype(o_ref.dtype)

    n_loop = (n_super - 1) // FLASH_UNROLL
    reset_state()
    cmax = stage(0, 0, bufs[0], None, None, None)
    for qg in range(q_groups):
        cmax = lax.fori_loop(0, n_loop, lambda i, cm, qg=qg: stages(qg, FLASH_UNROLL * i + 1, FLASH_UNROLL, cm),
                             cmax)
        cmax = stages(qg, FLASH_UNROLL * n_loop + 1, (n_super - 1) % FLASH_UNROLL, cmax)
        last_buf = bufs[(qg * n_super + n_super - 1) % 2]
        if qg + 1 < q_groups:
            cmax = stage(qg + 1, 0, bufs[((qg + 1) * n_super) % 2], n_super - 1, last_buf, cmax)
            finish(qg)
            reset_state()
        else:
            stage(None, None, None, n_super - 1, last_buf, cmax)
            finish(qg)


def flash_attention(qt, k, vt, *, group, q_tiles, q_groups, tks, q_row0, n_q, k_row0, n_k):
    hk, _, dq = k.shape
    dv = vt.shape[1]
    tq = ROW_TILE * q_tiles * q_groups
    assert q_row0 % tq == 0 and n_q % tq == 0 and k_row0 % n_k == 0 and n_k % tks == 0
    assert tks % KEY_CHUNK == 0 and FLASH_UNROLL % 2 == 0
    qb, kb = q_row0 // tq, k_row0 // n_k
    n_str = group * q_tiles
    return pl.pallas_call(
        functools.partial(_flash_kernel, group, q_tiles, q_groups, tks),
        grid=(hk, n_q // tq),
        in_specs=[pl.BlockSpec((group, dq, tq), lambda h, i: (h, 0, qb + i)),
                  pl.BlockSpec((1, n_k, dq), lambda h, i: (h, kb, 0)),
                  pl.BlockSpec((1, dv, n_k), lambda h, i: (h, 0, kb))],
        out_specs=pl.BlockSpec((tq, group * dv), lambda h, i: (i, h)),
        out_shape=jax.ShapeDtypeStruct((n_q, hk * group * dv), jnp.bfloat16),
        scratch_shapes=[pltpu.VMEM((n_str, tks, ROW_TILE), jnp.float32),
                        pltpu.VMEM((n_str, tks, ROW_TILE), jnp.float32),
                        pltpu.VMEM((n_str, 1, ROW_TILE), jnp.float32),
                        pltpu.VMEM((n_str, 1, ROW_TILE), jnp.float32),
                        pltpu.VMEM((n_str, dv, ROW_TILE), jnp.float32)],
        compiler_params=_cparams(("arbitrary", "arbitrary")),
        name="flash_attention",
    )(qt, k, vt)


def _gla_block(q_ref, k_ref, v_ref, c_ref, o_ref, s_ref, ge_ref, bd_ref, r0, forward):
    nb = GLA_BLOCK
    rows = pl.ds(r0, nb)
    q, k, v, b = q_ref[rows, :], k_ref[rows, :], v_ref[rows, :], c_ref[rows, :]
    total = b[nb - 1:nb, :] if forward else b[0:1, :]
    n_pair = s_ref.shape[0]
    wk, wv = 2 * GLA_DK, 2 * GLA_DV
    qe = (q * jnp.exp(b)).astype(jnp.bfloat16)
    o = jnp.concatenate([_dot_nt(qe[:, i * wk:(i + 1) * wk], s_ref[i].astype(jnp.bfloat16))
                         for i in range(n_pair)], axis=1)
    rid = lax.broadcasted_iota(jnp.int32, b.shape, 0)
    terms = []
    for j in range(nb):
        keep = (rid >= j) if forward else (rid <= j)
        w = jnp.exp(jnp.where(keep, b - b[j:j + 1, :], -jnp.inf))
        terms.append((q * w * k[j:j + 1, :]).astype(jnp.bfloat16))
    scores = _dot(jnp.concatenate(terms, axis=0), ge_ref[...])
    for j in range(nb):
        o = o + scores[j * nb:(j + 1) * nb, :] * v[j:j + 1, :]
    o_ref[rows, :] = o
    ke = (k * jnp.exp(total - b)).astype(jnp.bfloat16)
    vb = v.astype(jnp.bfloat16)
    decay = jnp.exp(total)
    for i in range(n_pair):
        upd = lax.dot_general(vb[:, i * wv:(i + 1) * wv], ke[:, i * wk:(i + 1) * wk],
                              (((0,), (0,)), ((), ())), preferred_element_type=jnp.float32)
        s_ref[i] = decay[:, i * wk:(i + 1) * wk] * s_ref[i] + bd_ref[...] * upd


def _gla_kernel(qf, kf, vf, mf, qb, kb, vb, mb, wgf, bgf, wgb, bgb, tri_f, tri_b, ge, bd,
                of, ob, cf_ref, cb_ref, sf_ref, sb_ref):
    @pl.when(pl.program_id(0) == 0)
    def _():
        sf_ref[...] = jnp.zeros(sf_ref.shape, jnp.float32)
        sb_ref[...] = jnp.zeros(sb_ref.shape, jnp.float32)

    def cum_log_decay(m_ref, w_ref, b_ref, tri_ref):
        z = _dot_split(m_ref[...], w_ref[...]) + b_ref[...]
        la = (jnp.minimum(z, 0.0) - jnp.log1p(jnp.exp(-jnp.abs(z)))) * (1.0 / GLA_GATE_NORM)
        la_hi, la_lo = _split_bf16(la)
        tri = tri_ref[...]
        return _dot(tri, la_hi) + _dot(tri, la_lo)

    cf_ref[...] = cum_log_decay(mf, wgf, bgf, tri_f)
    cb_ref[...] = cum_log_decay(mb, wgb, bgb, tri_b)
    n_blocks = qf.shape[0] // GLA_BLOCK

    def body(t, carry):
        r_f = pl.multiple_of(t * GLA_BLOCK, GLA_BLOCK)
        r_b = pl.multiple_of((n_blocks - 1 - t) * GLA_BLOCK, GLA_BLOCK)
        _gla_block(qf, kf, vf, cf_ref, of, sf_ref, ge, bd, r_f, True)
        _gla_block(qb, kb, vb, cb_ref, ob, sb_ref, ge, bd, r_b, False)
        return carry

    lax.fori_loop(0, n_blocks, body, 0)


def gla_scan(p, w_gf_p, b_gf, w_gb_p, b_gb, consts):
    N = p.shape[0]
    tm = ROW_TILE
    nt = N // tm
    nk, nv = GLA_HEADS * GLA_DK, GLA_HEADS * GLA_DV
    fwd = lambda i: (i + nt - 1) % nt
    bwd = lambda i: (2 * nt - 1 - i) % nt
    col = lambda w, c, order: pl.BlockSpec((tm, w), lambda i: (order(i), c))
    full = lambda a: pl.BlockSpec(a.shape, lambda i: (0,) * a.ndim)
    side = lambda order: [col(nk, COL_LQ // nk, order), col(nk, COL_LK // nk, order),
                          col(nv, COL_LV // nv, order), col(LANES, COL_MISC // LANES, order)]
    small = (w_gf_p, b_gf, w_gb_p, b_gb) + tuple(consts)
    return pl.pallas_call(
        _gla_kernel,
        grid=(nt,),
        in_specs=side(fwd) + side(bwd) + [full(a) for a in small],
        out_specs=[pl.BlockSpec((tm, nv), lambda i: (fwd(i), 0)),
                   pl.BlockSpec((tm, nv), lambda i: (bwd(i), 0))],
        out_shape=[jax.ShapeDtypeStruct((N, nv), jnp.float32)] * 2,
        scratch_shapes=[pltpu.VMEM((tm, nk), jnp.float32), pltpu.VMEM((tm, nk), jnp.float32),
                        pltpu.VMEM((GLA_HEADS // 2, 2 * GLA_DV, 2 * GLA_DK), jnp.float32),
                        pltpu.VMEM((GLA_HEADS // 2, 2 * GLA_DV, 2 * GLA_DK), jnp.float32)],
        compiler_params=_cparams(("arbitrary",)),
        name="gla_scan",
    )(p, p, p, p, p, p, p, p, *small)


def _gla_consts():
    i = jnp.arange(ROW_TILE)
    same = (i[:, None] // GLA_BLOCK) == (i[None, :] // GLA_BLOCK)
    tri_f = (same & (i[None, :] <= i[:, None])).astype(jnp.bfloat16)
    tri_b = (same & (i[None, :] >= i[:, None])).astype(jnp.bfloat16)
    hk = jnp.arange(GLA_HEADS * GLA_DK) // GLA_DK
    hv = jnp.arange(GLA_HEADS * GLA_DV) // GLA_DV
    ge = (hk[:, None] == hv[None, :]).astype(jnp.bfloat16)
    bd = (hv[:2 * GLA_DV, None] == hk[None, :2 * GLA_DK]).astype(jnp.float32)
    return tri_f, tri_b, ge, bd


def _prep_gate(w_f, b_f, w_b, b_b):
    z = jnp.zeros((LANES, w_f.shape[1]), jnp.float32)
    return (z.at[MLA_ROPE:MLA_ROPE + GLA_GATE_RANK].set(w_f), b_f[None],
            z.at[MLA_ROPE + GLA_GATE_RANK:MLA_ROPE + 2 * GLA_GATE_RANK].set(w_b), b_b[None])


def _layer_norm(u, g, b):
    mu = jnp.mean(u, axis=-1, keepdims=True)
    d = u - mu
    var = jnp.mean(d * d, axis=-1, keepdims=True)
    return d * lax.rsqrt(var + LN_EPS) * g + b


def _first_max(rows):
    best, idx = rows[0], jnp.zeros(rows[0].shape, jnp.int32)
    for i in range(1, len(rows)):
        better = rows[i] > best
        idx = jnp.where(better, i, idx)
        best = jnp.where(better, rows[i], best)
    return best, idx


def _route(lt):
    grp_rows = [lt[g:g + 1, :] for g in range(N_GROUPS)]
    g_best, grp = _first_max(grp_rows)
    grp_w = 1.0 / sum(jnp.exp(r - g_best) for r in grp_rows)
    in_grp = []
    for j in range(EXPERTS_PER_GROUP):
        val = lt[8 + j:9 + j, :]
        for g in range(1, N_GROUPS):
            r = 8 + g * EXPERTS_PER_GROUP + j
            val = jnp.where(grp == g, lt[r:r + 1, :], val)
        in_grp.append(val)
    v1, i1 = _first_max(in_grp)
    rest = [jnp.where(i1 == j, -jnp.inf, in_grp[j]) for j in range(EXPERTS_PER_GROUP)]
    v2, i2 = _first_max(rest)
    e = jnp.exp(v2 - v1)
    w1 = 1.0 / (1.0 + e)
    ids = jnp.concatenate([grp * EXPERTS_PER_GROUP + i1, grp * EXPERTS_PER_GROUP + i2], axis=0)
    gates = jnp.concatenate([grp_w * w1, grp_w * (e * w1)], axis=0)
    return ids, gates


def _outproj_kernel(n_lat_tiles, alpha,
                    x_ref, yml, ymc, ygl, ygc, of_ref, ob_ref, g_ref, on_ref, wo_ref,
                    g1_ref, sh2_ref, sc2_ref, lg_ref, lb_ref, wr_ref, br_ref,
                    x1_ref, h2_ref, id_ref, gt_ref):
    is_ctx = pl.program_id(0) >= n_lat_tiles
    y_mla = jnp.where(is_ctx, ymc[...], yml[...])
    y_gqa = jnp.where(is_ctx, ygc[...], ygl[...])
    o = of_ref[...] + ob_ref[...]
    heads = [_rms(o[:, h * GLA_DV:(h + 1) * GLA_DV], on_ref[...]) for h in range(GLA_HEADS)]
    g = g_ref[...]
    y_gla = (jnp.concatenate(heads, axis=1) * (g * jax.nn.sigmoid(g))).astype(jnp.bfloat16)
    w_mla, w_gqa = y_mla.shape[1], y_gqa.shape[1]
    y = (_dot(y_mla, wo_ref[0:w_mla, :]) + _dot(y_gqa, wo_ref[w_mla:w_mla + w_gqa, :])
         + _dot(y_gla, wo_ref[w_mla + w_gqa:, :]))
    x1 = _layer_norm(alpha * x_ref[...] + _mod_row(g1_ref, is_ctx) * y, lg_ref[...], lb_ref[...])
    x1_ref[...] = x1
    h2 = x1 * (1.0 + _mod_row(sc2_ref, is_ctx)) + _mod_row(sh2_ref, is_ctx)
    h2_ref[...] = h2
    lt = _dot_split(wr_ref[...], h2, _dot_nt) + br_ref[...]
    ids, gates = _route(lt)
    rid = lax.broadcasted_iota(jnp.int32, id_ref.shape, 0)
    id_ref[...] = jnp.where(rid == 0, ids[0:1], jnp.where(rid == 1, ids[1:2], 0))
    gt_ref[...] = jnp.where(rid == 0, gates[0:1], jnp.where(rid == 1, gates[1:2], 0.0))


def out_projection(xs, y_mla, y_gqa, o_f, o_b, p, out_norm, w_out, mod_l, ln_g, ln_b, w_rt, b_rt,
                   n_lat, alpha):
    N, D = xs.shape
    tm = ROW_TILE
    nl = n_lat // tm
    assert N - n_lat == tm
    nv = GLA_HEADS * GLA_DV
    lat = lambda a: pl.BlockSpec((tm, a.shape[1]), lambda i: (jnp.minimum(i, nl - 1), 0))
    ctx = lambda a: pl.BlockSpec((tm, a.shape[1]), lambda i: (0, 0))
    row = lambda w, c: pl.BlockSpec((tm, w), lambda i: (i, c))
    full = lambda a: pl.BlockSpec(a.shape, lambda i: (0,) * a.ndim)
    modc = lambda c: pl.BlockSpec((8, D), lambda i: (0, c))
    (yml, ymc), (ygl, ygc) = y_mla, y_gqa
    return pl.pallas_call(
        functools.partial(_outproj_kernel, nl, alpha),
        grid=(N // tm,),
        in_specs=[row(D, 0), lat(yml), ctx(ymc), lat(ygl), ctx(ygc), row(nv, 0), row(nv, 0),
                  row(nv, COL_LG // nv), full(out_norm), full(w_out),
                  modc(2), modc(3), modc(4), full(ln_g), full(ln_b), full(w_rt), full(b_rt)],
        out_specs=[row(D, 0), row(D, 0), pl.BlockSpec((8, tm), lambda i: (0, i)),
                   pl.BlockSpec((8, tm), lambda i: (0, i))],
        out_shape=[jax.ShapeDtypeStruct((N, D), jnp.float32), jax.ShapeDtypeStruct((N, D), jnp.float32),
                   jax.ShapeDtypeStruct((8, N), jnp.int32), jax.ShapeDtypeStruct((8, N), jnp.float32)],
        compiler_params=_cparams(("arbitrary",)),
        name="out_projection",
    )(xs, yml, ymc, ygl, ygc, o_f, o_b, p, out_norm, w_out, mod_l, mod_l, mod_l, ln_g, ln_b, w_rt, b_rt)


def _row_copy(src_hbm, idx_ref, dst, sem, r):
    return pltpu.make_async_copy(src_hbm.at[pl.ds(idx_ref[0, 0, r], 1), :], dst.at[pl.ds(r, 1), :], sem)


def _start_rows(src_hbm, idx_ref, dst, sem, unrolled):
    n = dst.shape[0]
    if unrolled:
        for r in range(n):
            _row_copy(src_hbm, idx_ref, dst, sem, r).start()
    else:
        def start(r, c):
            _row_copy(src_hbm, idx_ref, dst, sem, r).start()
            return c
        lax.fori_loop(0, n, start, 0, unroll=8)


def _wait_rows(src_hbm, dst, sem):
    pltpu.make_async_copy(src_hbm.at[pl.ds(0, dst.shape[0]), :], dst, sem).wait()


def _double_buffered_gather(i, n_steps, src_hbm, idx_ref, idx_next_ref, bufs, sems, compute):
    @pl.when(i == 0)
    def _():
        _start_rows(src_hbm, idx_ref, bufs[0], sems.at[0], unrolled=False)

    for parity in (0, 1):
        cur, nxt = bufs[parity], bufs[1 - parity]
        for has_next in (True, False):
            more = i + 1 < n_steps
            cond = (i < n_steps) & (i % 2 == parity) & (more if has_next else jnp.logical_not(more))

            @pl.when(cond)
            def _(cur=cur, nxt=nxt, parity=parity, has_next=has_next):
                _wait_rows(src_hbm, cur, sems.at[parity])
                if has_next:
                    _start_rows(src_hbm, idx_next_ref, nxt, sems.at[1 - parity], unrolled=True)
                compute(cur)


def _expert_kernel(te_ref, nu_ref, idx_ref, idx_next_ref, h_hbm, wg_ref, wu_ref, wd_ref, y_ref,
                   xa, xb, wg_bf, wu_bf, wd_bf, sems):
    i = pl.program_id(0)
    n_used = nu_ref[0]

    @pl.when((i < n_used) & ((i == 0) | (te_ref[i] != te_ref[jnp.maximum(i - 1, 0)])))
    def _():
        wg_bf[...] = wg_ref[0, 0].astype(jnp.bfloat16)
        wu_bf[...] = wu_ref[0, 0].astype(jnp.bfloat16)
        wd_bf[...] = wd_ref[0, 0].astype(jnp.bfloat16)

    def compute(x_ref):
        x = x_ref[...].astype(jnp.bfloat16)
        a = _dot(x, wg_bf[...])
        u = _dot(x, wu_bf[...])
        hid = (a * jax.nn.sigmoid(a) * u).astype(jnp.bfloat16)
        y_ref[...] = _dot(hid, wd_bf[...])

    _double_buffered_gather(i, n_used, h_hbm, idx_ref, idx_next_ref, (xa, xb), sems, compute)

    @pl.when(i >= n_used)
    def _():
        y_ref[...] = jnp.zeros(y_ref.shape, jnp.float32)


def expert_ffn(h2, tile_expert, n_used, sorted_tok, w_g, w_u, w_d, layer):
    N, D = h2.shape
    tm = MOE_TILE
    n_tiles = sorted_tok.shape[0] // tm
    F = w_g.shape[3]
    bf = jnp.bfloat16
    idx = sorted_tok.reshape(n_tiles, 1, tm)
    nxt = lambda i, te, nu: (jnp.minimum(i + 1, n_tiles - 1), 0, 0)
    gs = pltpu.PrefetchScalarGridSpec(
        num_scalar_prefetch=2,
        grid=(n_tiles,),
        in_specs=[pl.BlockSpec((1, 1, tm), lambda i, te, nu: (i, 0, 0), memory_space=pltpu.SMEM),
                  pl.BlockSpec((1, 1, tm), nxt, memory_space=pltpu.SMEM),
                  pl.BlockSpec(memory_space=pl.ANY),
                  pl.BlockSpec((1, 1, D, F), lambda i, te, nu: (layer, te[i], 0, 0)),
                  pl.BlockSpec((1, 1, D, F), lambda i, te, nu: (layer, te[i], 0, 0)),
                  pl.BlockSpec((1, 1, F, D), lambda i, te, nu: (layer, te[i], 0, 0))],
        out_specs=pl.BlockSpec((tm, D), lambda i, te, nu: (i, 0)),
        scratch_shapes=[pltpu.VMEM((tm, D), jnp.float32), pltpu.VMEM((tm, D), jnp.float32),
                        pltpu.VMEM((D, F), bf), pltpu.VMEM((D, F), bf), pltpu.VMEM((F, D), bf),
                        pltpu.SemaphoreType.DMA((2,))],
    )
    return pl.pallas_call(
        _expert_kernel,
        grid_spec=gs,
        out_shape=jax.ShapeDtypeStruct((n_tiles * tm, D), jnp.float32),
        compiler_params=_cparams(("arbitrary",)),
        name="expert_ffn",
    )(tile_expert, n_used, idx, idx, h2, w_g, w_u, w_d)


def _combine_kernel(n_lat_tiles, alpha, pos_ref, pos_next_ref, y_hbm, x_ref, gc_ref, g2_ref, lg_ref,
                    lb_ref, o_ref, ya, yb, sems):
    i = pl.program_id(0)
    is_ctx = i >= n_lat_tiles
    tm = x_ref.shape[0]

    def compute(y_ref):
        gc = gc_ref[...]
        f = gc[:, 0:1] * y_ref[0:tm, :] + gc[:, 1:2] * y_ref[tm:2 * tm, :]
        o_ref[...] = _layer_norm(alpha * x_ref[...] + _mod_row(g2_ref, is_ctx) * f, lg_ref[...], lb_ref[...])

    _double_buffered_gather(i, pl.num_programs(0), y_hbm, pos_ref, pos_next_ref, (ya, yb), sems, compute)


def moe_combine(xs1, y_sorted, pos, gate_cols, mod_l, ln_g, ln_b, n_lat, alpha):
    N, D = xs1.shape
    tm = ROW_TILE
    nt = N // tm
    full = lambda a: pl.BlockSpec(a.shape, lambda i: (0,) * a.ndim)
    return pl.pallas_call(
        functools.partial(_combine_kernel, n_lat // tm, alpha),
        grid=(nt,),
        in_specs=[pl.BlockSpec((1, 1, 2 * tm), lambda i: (i, 0, 0), memory_space=pltpu.SMEM),
                  pl.BlockSpec((1, 1, 2 * tm), lambda i: (jnp.minimum(i + 1, nt - 1), 0, 0),
                               memory_space=pltpu.SMEM),
                  pl.BlockSpec(memory_space=pl.ANY),
                  pl.BlockSpec((tm, D), lambda i: (i, 0)),
                  pl.BlockSpec((tm, 2), lambda i: (i, 0)),
                  pl.BlockSpec((8, D), lambda i: (0, 5)), full(ln_g), full(ln_b)],
        out_specs=pl.BlockSpec((tm, D), lambda i: (i, 0)),
        out_shape=jax.ShapeDtypeStruct((N, D), jnp.float32),
        scratch_shapes=[pltpu.VMEM((2 * tm, D), jnp.float32), pltpu.VMEM((2 * tm, D), jnp.float32),
                        pltpu.SemaphoreType.DMA((2,))],
        compiler_params=_cparams(("arbitrary",)),
        name="moe_combine",
    )(pos, pos, y_sorted, xs1, gate_cols, mod_l, ln_g, ln_b)


def _dispatch_plan(ids, n_tok):
    tm = MOE_TILE
    n_asg = 2 * n_tok
    assert n_asg * n_asg < 2 ** 31
    n_tiles = n_asg // tm + N_EXPERTS
    i32 = jnp.int32
    e_flat = ids.reshape(n_asg)
    ar = jnp.arange(n_asg, dtype=i32)
    order = jnp.sort(e_flat * n_asg + ar) % n_asg
    rank = jnp.sort(order * n_asg + ar) % n_asg
    experts = jnp.arange(N_EXPERTS, dtype=i32)
    onehot = e_flat[:, None] == experts[None, :]
    counts = jnp.sum(onehot, axis=0, dtype=i32)
    ends = jnp.cumsum(counts).astype(i32)
    padded = ((counts + tm - 1) // tm) * tm
    pad_ends = jnp.cumsum(padded).astype(i32)
    shift = (pad_ends - padded) - (ends - counts)
    pos = rank + jnp.sum(jnp.where(onehot, shift[None, :], 0), axis=1, dtype=i32)
    n_used = (pad_ends[-1] // tm).reshape(1)
    tile_start = jnp.arange(n_tiles, dtype=i32) * tm
    tile_expert = jnp.sum(tile_start[:, None] >= pad_ends[None, :], axis=1, dtype=i32)
    last = jnp.sum((pad_ends[-1] - 1) >= pad_ends, dtype=i32)
    tile_expert = jnp.minimum(tile_expert, last)
    of_tile = tile_expert[:, None] == experts[None, :]
    shift_t = jnp.sum(jnp.where(of_tile, shift[None, :], 0), axis=1, dtype=i32)
    ends_t = jnp.sum(jnp.where(of_tile, ends[None, :], 0), axis=1, dtype=i32)
    r = tile_start[:, None] + jnp.arange(tm, dtype=i32)[None, :] - shift_t[:, None]
    valid = (r < ends_t[:, None]) & (tile_start[:, None] < pad_ends[-1])
    sorted_tok = jnp.where(valid, order[jnp.clip(r, 0, n_asg - 1)] % n_tok, 0).reshape(n_tiles * tm)
    nt = n_tok // ROW_TILE
    pos_tiles = jnp.concatenate([pos[:n_tok].reshape(nt, 1, ROW_TILE),
                                 pos[n_tok:].reshape(nt, 1, ROW_TILE)], axis=2)
    return sorted_tok, tile_expert, n_used, pos_tiles


def _rope_tables(n_lat, n_ctx, dim):
    rows = n_lat // GRID_W
    row = jnp.repeat(jnp.arange(rows), GRID_W).astype(jnp.float32)
    col = jnp.tile(jnp.arange(GRID_W), rows).astype(jnp.float32)
    half = dim // 2
    inv_freq = ROPE_THETA ** (-jnp.arange(0, half, 2, dtype=jnp.float32) / half)
    ang_r = row[:, None] * inv_freq
    ang_c = col[:, None] * inv_freq
    ang = jnp.concatenate([ang_r, ang_r, ang_c, ang_c], axis=-1)
    cos, sin = jnp.cos(ang), jnp.sin(ang)
    q = dim // 4
    lane = jnp.arange(dim)
    first = ((lane // q) % 2) == 0
    sin_a = jnp.where(first, -sin, 0.0)
    sin_b = jnp.where(first, 0.0, sin)

    def pad(t, fill):
        t = jnp.pad(t, ((0, 0), (0, LANES - dim)), constant_values=fill)
        return jnp.pad(t, ((0, n_ctx), (0, 0)), constant_values=fill)

    return pad(cos, 1.0), pad(sin_a, 0.0), pad(sin_b, 0.0)


def _in_perm():
    a = MLA_Q_RANK + MLA_KV_RANK + MLA_ROPE
    b = a + (GQA_HEADS + 2 * GQA_KV_HEADS) * HEAD_DIM
    nk, nv = GLA_HEADS * GLA_DK, GLA_HEADS * GLA_DV
    r = lambda s, n: list(range(s, s + n))
    cols = (r(0, MLA_Q_RANK + MLA_KV_RANK)
            + r(a, b - a)
            + r(b, 2 * nk + 2 * nv)
            + r(MLA_Q_RANK + MLA_KV_RANK, MLA_ROPE)
            + r(b + 2 * nk + 2 * nv, 2 * GLA_GATE_RANK))
    return jnp.asarray(cols, jnp.int32)


def _prep_w_in(w_in_l):
    w = w_in_l[:, _in_perm()]
    scale = jnp.ones((w.shape[1],), jnp.float32).at[COL_LQ:COL_LK].set(GLA_DK ** -0.5)
    w = w * scale
    return jnp.pad(w, ((0, 0), (0, P_COLS - w.shape[1]))).astype(jnp.bfloat16)


def _prep_w_uq(w_uq_l):
    r = w_uq_l.shape[0]
    w = w_uq_l.reshape(r, MLA_HEADS, MLA_NOPE + MLA_ROPE)
    w = jnp.pad(w, ((0, 0), (0, 0), (0, MLA_QPAD - MLA_NOPE - MLA_ROPE)))
    return w.reshape(r, MLA_HEADS * MLA_QPAD).astype(jnp.bfloat16)


def _prep_router(w_rg, b_rg, w_re, b_re):
    D = w_rg.shape[0]
    w = jnp.zeros((LANES, D), jnp.float32).at[0:N_GROUPS].set(w_rg.T).at[8:8 + N_EXPERTS].set(w_re.T)
    b = jnp.zeros((LANES,), jnp.float32).at[0:N_GROUPS].set(b_rg).at[8:8 + N_EXPERTS].set(b_re)
    return w, b[:, None]


def _q_split(n_lat, want_tiles):
    tiles = n_lat // ROW_TILE
    q_tiles = want_tiles if tiles % want_tiles == 0 else 1
    q_groups = max(g for g in (4, 2, 1) if (tiles // q_tiles) % g == 0)
    return dict(q_tiles=q_tiles, q_groups=q_groups)


def _kv_tile(n):
    return max(t for t in (256, 640, 1280) if n % t == 0)


def kernel(x, c, ctx, c_ctx, ada_w, ada_b, w_in, mla_q_norm, mla_w_uq, mla_kv_norm, mla_w_ukv, gqa_q_norm, gqa_k_norm, gla_w_gate_fwd, gla_b_gate_fwd, gla_w_gate_bwd, gla_b_gate_bwd, gla_out_norm, w_out, ln1_g, ln1_b, w_route_group, b_route_group, w_route_expert, b_route_expert, w_expert_gate, w_expert_up, w_expert_down, ln2_g, ln2_b):
    T, C = x.shape[1], ctx.shape[1]
    N = T + C
    L = ada_w.shape[0]
    assert x.shape[0] == 1 and C == ROW_TILE and T % ROW_TILE == 0 and T % GRID_W == 0
    alpha = (2.0 * L) ** 0.25
    bf = jnp.bfloat16

    xs = jnp.concatenate([x[0], ctx[0]], axis=0)
    mods = modulation(c, c_ctx, ada_w, ada_b)
    tabs_mla = _rope_tables(T, C, MLA_ROPE)
    tabs_gqa = _rope_tables(T, C, HEAD_DIM)
    gla_consts = _gla_consts()
    mla_scale = (MLA_NOPE + MLA_ROPE) ** -0.5 * LOG2E
    gqa_scale = HEAD_DIM ** -0.5 * LOG2E
    tabs_mla_t = tuple(t.T for t in tabs_mla)
    tabs_gqa_t = tuple(t.T for t in tabs_gqa)
    lat = dict(q_row0=0, n_q=T, k_row0=0, n_k=N, tks=_kv_tile(N))
    con = dict(q_row0=T, n_q=C, k_row0=T, n_k=C, tks=C)

    for l in range(L):
        mod_l = mods[l]
        p = in_projection(xs, mod_l, _prep_w_in(w_in[l]), T)

        w_ukv = mla_w_ukv[l].reshape(MLA_KV_RANK, MLA_HEADS, MLA_NOPE + MLA_V)
        w_kn = w_ukv[:, :, :MLA_NOPE].reshape(MLA_KV_RANK, MLA_HEADS * MLA_NOPE).astype(bf)
        w_v_t = w_ukv[:, :, MLA_NOPE:].reshape(MLA_KV_RANK, MLA_HEADS * MLA_V).T.astype(bf)
        qt, k, vt = mla_prep(p, (mla_q_norm[l] * mla_scale)[None], _prep_w_uq(mla_w_uq[l]).T,
                             mla_kv_norm[l][None], w_kn, w_v_t, tabs_mla, tabs_mla_t)
        y_mla = (flash_attention(qt, k, vt, group=1, **_q_split(T, 2), **lat),
                 flash_attention(qt, k, vt, group=1, q_tiles=1, q_groups=1, **con))
        qt, k, vt = gqa_prep(p, (gqa_q_norm[l] * gqa_scale)[:, None], gqa_k_norm[l][None],
                             tabs_gqa, tabs_gqa_t)
        y_gqa = (flash_attention(qt, k, vt, group=GQA_GROUP, **_q_split(T, 1), **lat),
                 flash_attention(qt, k, vt, group=GQA_GROUP, q_tiles=1, q_groups=1, **con))
        o_f, o_b = gla_scan(p, *_prep_gate(gla_w_gate_fwd[l], gla_b_gate_fwd[l],
                                           gla_w_gate_bwd[l], gla_b_gate_bwd[l]), gla_consts)

        w_rt, b_rt = _prep_router(w_route_group[l], b_route_group[l], w_route_expert[l], b_route_expert[l])
        x1, h2, ids, gates = out_projection(xs, y_mla, y_gqa, o_f, o_b, p, gla_out_norm[l][None],
                                            w_out[l].astype(bf), mod_l, ln1_g[l][None], ln1_b[l][None],
                                            w_rt, b_rt, T, alpha)

        sorted_tok, tile_expert, n_used, pos_tiles = _dispatch_plan(ids[0:2], N)
        y_sorted = expert_ffn(h2, tile_expert, n_used, sorted_tok, w_expert_gate, w_expert_up,
                              w_expert_down, l)
        xs = moe_combine(x1, y_sorted, pos_tiles, gates[0:2].T, mod_l, ln2_g[l][None], ln2_b[l][None],
                         T, alpha)
    return xs[:T][None]
```

```python
import functools
import math

import jax
import jax.numpy as jnp
from jax import lax
from jax.experimental import pallas as pl
from jax.experimental.pallas import tpu as pltpu

GRID_W = 64
ROPE_THETA = 10000.0
NORM_EPS = 1e-6
LN_EPS = 1e-5
HEAD_DIM = 128
MLA_HEADS = 6
MLA_Q_RANK = 512
MLA_KV_RANK = 256
MLA_NOPE = 128
MLA_ROPE = 64
MLA_V = 128
GQA_HEADS = 6
GQA_KV_HEADS = 2
GQA_GROUP = GQA_HEADS // GQA_KV_HEADS
GLA_HEADS = 4
GLA_DK = 64
GLA_DV = 128
GLA_GATE_RANK = 16
GLA_GATE_NORM = 16.0
N_GROUPS = 4
EXPERTS_PER_GROUP = 8
N_EXPERTS = N_GROUPS * EXPERTS_PER_GROUP
N_MOD = 6

LANES = 128
VMEM_LIMIT = 48 * 1024 * 1024

ROW_TILE = 256
GLA_BLOCK = 16
MOE_TILE = 256

COL_CQ = 0
COL_CKV = 512
COL_GQ = 768
COL_GK = 1536
COL_GV = 1792
COL_LQ = 2048
COL_LK = 2304
COL_LV = 2560
COL_LG = 3072
COL_MISC = 3584
P_COLS = 3840
P_HALF = P_COLS // 2
MLA_QPAD = 256

LOG2E = math.log2(math.e)


def _cparams(sem):
    return pltpu.CompilerParams(dimension_semantics=sem, vmem_limit_bytes=VMEM_LIMIT)


def _dot(a, b):
    return jnp.dot(a, b, preferred_element_type=jnp.float32)


def _dot_nt(a, b):
    return lax.dot_general(a, b, (((1,), (1,)), ((), ())), preferred_element_type=jnp.float32)


def _split_bf16(x):
    hi = x.astype(jnp.bfloat16)
    return hi, (x - hi.astype(jnp.float32)).astype(jnp.bfloat16)


def _dot_split(a, b, dot=_dot):
    a_hi, a_lo = _split_bf16(a)
    b_hi, b_lo = _split_bf16(b)
    return dot(a_hi, b_hi) + dot(a_lo, b_hi) + dot(a_hi, b_lo)


def _mod_kernel(crep_ref, w_ref, b_ref, o_ref):
    tn = w_ref.shape[2]
    rows = []
    for r in range(2):
        cr = crep_ref[r]
        a = cr * jax.nn.sigmoid(cr)
        parts = []
        for j in range(tn // LANES):
            wj = w_ref[0, :, j * LANES:(j + 1) * LANES]
            parts.append(jnp.sum(wj * a, axis=0, keepdims=True))
        rows.append(jnp.concatenate(parts, axis=1) + b_ref[0])
    rid = lax.broadcasted_iota(jnp.int32, (8, tn), 0)
    o_ref[0] = jnp.where(rid == 0, rows[0], jnp.where(rid == 1, rows[1], 0.0))


def modulation(c, c_ctx, ada_w, ada_b):
    L, D, M = ada_w.shape
    tn = max(t for t in range(LANES, 1024 + 1, LANES) if M % t == 0)
    crep = jnp.broadcast_to(jnp.stack([c[0], c_ctx])[:, :, None], (2, D, LANES))
    return pl.pallas_call(
        _mod_kernel,
        grid=(L, M // tn),
        in_specs=[pl.BlockSpec((2, D, LANES), lambda l, j: (0, 0, 0)),
                  pl.BlockSpec((1, D, tn), lambda l, j: (l, 0, j)),
                  pl.BlockSpec((1, 1, tn), lambda l, j: (l, 0, j))],
        out_specs=pl.BlockSpec((1, 8, tn), lambda l, j: (l, 0, j)),
        out_shape=jax.ShapeDtypeStruct((L, 8, M), jnp.float32),
        compiler_params=_cparams(("arbitrary", "arbitrary")),
        name="modulation",
    )(crep, ada_w, ada_b.reshape(L, 1, M))


def _mod_row(ref, is_ctx):
    return jnp.where(is_ctx, ref[1:2, :], ref[0:1, :])


def _inproj_kernel(n_lat_tiles, x_ref, sh_ref, sc_ref, w_ref, o_ref):
    is_ctx = pl.program_id(1) >= n_lat_tiles
    h = x_ref[...] * (1.0 + _mod_row(sc_ref, is_ctx)) + _mod_row(sh_ref, is_ctx)
    o_ref[...] = _dot(h.astype(jnp.bfloat16), w_ref[...])


def in_projection(xs, mod_l, w_in_p, n_lat):
    N, D = xs.shape
    tm = ROW_TILE
    return pl.pallas_call(
        functools.partial(_inproj_kernel, n_lat // tm),
        grid=(2, N // tm),
        in_specs=[pl.BlockSpec((tm, D), lambda j, i: (i, 0)),
                  pl.BlockSpec((8, D), lambda j, i: (0, 0)),
                  pl.BlockSpec((8, D), lambda j, i: (0, 1)),
                  pl.BlockSpec((D, P_HALF), lambda j, i: (0, j))],
        out_specs=pl.BlockSpec((tm, P_HALF), lambda j, i: (i, j)),
        out_shape=jax.ShapeDtypeStruct((N, P_COLS), jnp.float32),
        compiler_params=_cparams(("arbitrary", "arbitrary")),
        name="in_projection",
    )(xs, mod_l, mod_l, w_in_p)


def _rope(x, cos, sin_a, sin_b, quarter):
    w = x.shape[-1]
    return x * cos + pltpu.roll(x, w - quarter, 1) * sin_a + pltpu.roll(x, quarter, 1) * sin_b


def _rms(x, gain):
    return x * lax.rsqrt(jnp.mean(x * x, axis=-1, keepdims=True) + NORM_EPS) * gain


def _rope_t(x, cos, sin_a, sin_b, quarter):
    w = x.shape[0]
    return x * cos + pltpu.roll(x, w - quarter, 0) * sin_a + pltpu.roll(x, quarter, 0) * sin_b


def _mla_prep_kernel(cq_ref, ckv_ref, misc_ref, qn_ref, wuqt_ref, kvn_ref, wkn_ref, wvt_ref,
                     cos_ref, sa_ref, sb_ref, cost_ref, sat_ref, sbt_ref, qt_ref, k_ref, vt_ref):
    cq = _rms(cq_ref[...], qn_ref[...]).astype(jnp.bfloat16)
    qt_all = _dot_nt(wuqt_ref[...], cq)
    ckv = _rms(ckv_ref[...], kvn_ref[...]).astype(jnp.bfloat16)
    kn_all = _dot(ckv, wkn_ref[...])
    vt_all = _dot_nt(wvt_ref[...], ckv)
    lane = lax.broadcasted_iota(jnp.int32, misc_ref.shape, 1)
    k_rope = _rope(misc_ref[...], cos_ref[...], sa_ref[...], sb_ref[...], MLA_ROPE // 4)
    k_rope = jnp.where(lane < MLA_ROPE, k_rope, 0.0).astype(jnp.bfloat16)
    cost, sat, sbt = cost_ref[...], sat_ref[...], sbt_ref[...]
    for h in range(MLA_HEADS):
        b0 = h * MLA_QPAD
        qt_ref[h, 0:LANES, :] = qt_all[b0:b0 + LANES, :].astype(jnp.bfloat16)
        qt_ref[h, LANES:2 * LANES, :] = _rope_t(qt_all[b0 + LANES:b0 + 2 * LANES, :], cost, sat, sbt,
                                                MLA_ROPE // 4).astype(jnp.bfloat16)
        k_ref[h, :, 0:LANES] = kn_all[:, h * MLA_NOPE:(h + 1) * MLA_NOPE].astype(jnp.bfloat16)
        k_ref[h, :, LANES:2 * LANES] = k_rope
        vt_ref[h] = vt_all[h * MLA_V:(h + 1) * MLA_V, :].astype(jnp.bfloat16)


def mla_prep(p, q_gain, w_uq_t, kv_gain, w_kn, w_v_t, tabs, tabs_t):
    N = p.shape[0]
    tm = ROW_TILE
    H = MLA_HEADS
    row = lambda w, c: pl.BlockSpec((tm, w), lambda i: (i, c))
    full = lambda a: pl.BlockSpec(a.shape, lambda i: (0,) * a.ndim)
    tab = pl.BlockSpec((tm, LANES), lambda i: (i, 0))
    tab_t = pl.BlockSpec((LANES, tm), lambda i: (0, i))
    return pl.pallas_call(
        _mla_prep_kernel,
        grid=(N // tm,),
        in_specs=[row(MLA_Q_RANK, COL_CQ // MLA_Q_RANK), row(MLA_KV_RANK, COL_CKV // MLA_KV_RANK),
                  row(LANES, COL_MISC // LANES),
                  full(q_gain), full(w_uq_t), full(kv_gain), full(w_kn), full(w_v_t),
                  tab, tab, tab, tab_t, tab_t, tab_t],
        out_specs=[pl.BlockSpec((H, MLA_QPAD, tm), lambda i: (0, 0, i)),
                   pl.BlockSpec((H, tm, MLA_QPAD), lambda i: (0, i, 0)),
                   pl.BlockSpec((H, MLA_V, tm), lambda i: (0, 0, i))],
        out_shape=[jax.ShapeDtypeStruct((H, MLA_QPAD, N), jnp.bfloat16),
                   jax.ShapeDtypeStruct((H, N, MLA_QPAD), jnp.bfloat16),
                   jax.ShapeDtypeStruct((H, MLA_V, N), jnp.bfloat16)],
        compiler_params=_cparams(("arbitrary",)),
        name="mla_prep",
    )(p, p, p, q_gain, w_uq_t, kv_gain, w_kn, w_v_t, *tabs, *tabs_t)


def _gqa_prep_kernel(q_in, k_in, v_in, qn_ref, kn_ref, cos_ref, sa_ref, sb_ref,
                     cost_ref, sat_ref, sbt_ref, qt_ref, k_ref, vt_ref):
    cos, sa, sb = cos_ref[...], sa_ref[...], sb_ref[...]
    cost, sat, sbt = cost_ref[...], sat_ref[...], sbt_ref[...]
    qt = HEAD_DIM // 4
    for h in range(GQA_HEADS):
        xt = q_in[:, h * HEAD_DIM:(h + 1) * HEAD_DIM].T
        xt = xt * lax.rsqrt(jnp.mean(xt * xt, axis=0, keepdims=True) + NORM_EPS) * qn_ref[...]
        qt_ref[h] = _rope_t(xt, cost, sat, sbt, qt).astype(jnp.bfloat16)
    for h in range(GQA_KV_HEADS):
        x = _rms(k_in[:, h * HEAD_DIM:(h + 1) * HEAD_DIM], kn_ref[...])
        k_ref[h] = _rope(x, cos, sa, sb, qt).astype(jnp.bfloat16)
        vt_ref[h] = v_in[:, h * HEAD_DIM:(h + 1) * HEAD_DIM].T.astype(jnp.bfloat16)


def gqa_prep(p, q_gain_col, k_gain, tabs, tabs_t):
    N = p.shape[0]
    tm = ROW_TILE
    wq, wk = GQA_HEADS * HEAD_DIM, GQA_KV_HEADS * HEAD_DIM
    row = lambda w, c: pl.BlockSpec((tm, w), lambda i: (i, c))
    full = lambda a: pl.BlockSpec(a.shape, lambda i: (0,) * a.ndim)
    tab = pl.BlockSpec((tm, LANES), lambda i: (i, 0))
    tab_t = pl.BlockSpec((LANES, tm), lambda i: (0, i))
    return pl.pallas_call(
        _gqa_prep_kernel,
        grid=(N // tm,),
        in_specs=[row(wq, COL_GQ // wq), row(wk, COL_GK // wk), row(wk, COL_GV // wk),
                  full(q_gain_col), full(k_gain), tab, tab, tab, tab_t, tab_t, tab_t],
        out_specs=[pl.BlockSpec((GQA_HEADS, HEAD_DIM, tm), lambda i: (0, 0, i)),
                   pl.BlockSpec((GQA_KV_HEADS, tm, HEAD_DIM), lambda i: (0, i, 0)),
                   pl.BlockSpec((GQA_KV_HEADS, HEAD_DIM, tm), lambda i: (0, 0, i))],
        out_shape=[jax.ShapeDtypeStruct((GQA_HEADS, HEAD_DIM, N), jnp.bfloat16),
                   jax.ShapeDtypeStruct((GQA_KV_HEADS, N, HEAD_DIM), jnp.bfloat16),
                   jax.ShapeDtypeStruct((GQA_KV_HEADS, HEAD_DIM, N), jnp.bfloat16)],
        compiler_params=_cparams(("arbitrary",)),
        name="gqa_prep",
    )(p, p, p, q_gain_col, k_gain, *tabs, *tabs_t)


KEY_CHUNK = 256
FLASH_UNROLL = 2


def _sublane_fold(x, op):
    return op(x.reshape(x.shape[0] // 8, 8, x.shape[1]), axis=0)


def _flash_kernel(group, q_tiles, q_groups, tks, qt_ref, k_ref, vt_ref, o_ref, s_even, s_odd,
                  m_ref, l_ref, acc_ref):
    dv = vt_ref.shape[1]
    tq = ROW_TILE
    n_chunks = tks // KEY_CHUNK
    n_super = k_ref.shape[1] // tks
    streams = [(g, t) for g in range(group) for t in range(q_tiles)]

    def reset_state():
        m_ref[...] = jnp.full(m_ref.shape, -jnp.inf, jnp.float32)
        l_ref[...] = jnp.zeros(l_ref.shape, jnp.float32)
        acc_ref[...] = jnp.zeros(acc_ref.shape, jnp.float32)

    def stage(qg_a, j_a, buf_a, j_b, buf_b, cmax_b):
        if j_a is not None:
            col0 = qg_a * q_tiles * tq
            ka = pl.multiple_of(j_a * tks, tks)
            cmax_a = [None] * len(streams)
        if j_b is not None:
            kb = pl.multiple_of(j_b * tks, tks)
            m_old = [m_ref[s] for s in range(len(streams))]
            m_new = [jnp.maximum(m_old[s], jnp.max(cmax_b[s], axis=0, keepdims=True))
                     for s in range(len(streams))]
            lsum = [jnp.zeros((8, tq), jnp.float32) for _ in streams]
            pv = [None] * len(streams)
        for c in range(n_chunks):
            rows = slice(c * KEY_CHUNK, (c + 1) * KEY_CHUNK)
            if j_a is not None:
                k_c = k_ref[0, pl.ds(ka + c * KEY_CHUNK, KEY_CHUNK), :]
            if j_b is not None:
                vt_c = vt_ref[0, :, pl.ds(kb + c * KEY_CHUNK, KEY_CHUNK)]
            for s, (g, t) in enumerate(streams):
                if j_a is not None:
                    sc = _dot(k_c, qt_ref[g, :, col0 + t * tq:col0 + (t + 1) * tq])
                    buf_a[s, rows, :] = sc
                    cm = _sublane_fold(sc, jnp.max)
                    cmax_a[s] = cm if cmax_a[s] is None else jnp.maximum(cmax_a[s], cm)
                if j_b is not None:
                    p = jnp.exp2(buf_b[s, rows, :] - m_new[s])
                    lsum[s] = lsum[s] + _sublane_fold(p, jnp.sum)
                    d = _dot(vt_c, p.astype(jnp.bfloat16))
                    pv[s] = d if pv[s] is None else d + pv[s]
        if j_b is not None:
            for s in range(len(streams)):
                alpha = jnp.exp2(m_old[s] - m_new[s])
                l_ref[s] = alpha * l_ref[s] + jnp.sum(lsum[s], axis=0, keepdims=True)
                acc_ref[s] = alpha * acc_ref[s] + pv[s]
                m_ref[s] = m_new[s]
        return tuple(cmax_a) if j_a is not None else None

    bufs = (s_even, s_odd)

    def stages(qg, j0, count, cm):
        base = qg * n_super
        for d in range(count):
            cm = stage(qg, j0 + d, bufs[(base + 1 + d) % 2], j0 + d - 1, bufs[(base + d) % 2], cm)
        return cm

    def finish(qg):
        row0 = qg * q_tiles * tq
        for s, (g, t) in enumerate(streams):
            o_ref[row0 + t * tq:row0 + (t + 1) * tq, g * dv:(g + 1) * dv] = (
                acc_ref[s] / l_ref[s]).T.astype(o_ref.dtype)

    n_loop = (n_super - 1) // FLASH_UNROLL
    reset_state()
    cmax = stage(0, 0, bufs[0], None, None, None)
    for qg in range(q_groups):
        cmax = lax.fori_loop(0, n_loop, lambda i, cm, qg=qg: stages(qg, FLASH_UNROLL * i + 1, FLASH_UNROLL, cm),
                             cmax)
        cmax = stages(qg, FLASH_UNROLL * n_loop + 1, (n_super - 1) % FLASH_UNROLL, cmax)
        last_buf = bufs[(qg * n_super + n_super - 1) % 2]
        if qg + 1 < q_groups:
            cmax = stage(qg + 1, 0, bufs[((qg + 1) * n_super) % 2], n_super - 1, last_buf, cmax)
            finish(qg)
            reset_state()
        else:
            stage(None, None, None, n_super - 1, last_buf, cmax)
            finish(qg)


def flash_attention(qt, k, vt, *, group, q_tiles, q_groups, tks, q_row0, n_q, k_row0, n_k):
    hk, _, dq = k.shape
    dv = vt.shape[1]
    tq = ROW_TILE * q_tiles * q_groups
    assert q_row0 % tq == 0 and n_q % tq == 0 and k_row0 % n_k == 0 and n_k % tks == 0
    assert tks % KEY_CHUNK == 0 and FLASH_UNROLL % 2 == 0
    qb, kb = q_row0 // tq, k_row0 // n_k
    n_str = group * q_tiles
    return pl.pallas_call(
        functools.partial(_flash_kernel, group, q_tiles, q_groups, tks),
        grid=(hk, n_q // tq),
        in_specs=[pl.BlockSpec((group, dq, tq), lambda h, i: (h, 0, qb + i)),
                  pl.BlockSpec((1, n_k, dq), lambda h, i: (h, kb, 0)),
                  pl.BlockSpec((1, dv, n_k), lambda h, i: (h, 0, kb))],
        out_specs=pl.BlockSpec((tq, group * dv), lambda h, i: (i, h)),
        out_shape=jax.ShapeDtypeStruct((n_q, hk * group * dv), jnp.bfloat16),
        scratch_shapes=[pltpu.VMEM((n_str, tks, ROW_TILE), jnp.float32),
                        pltpu.VMEM((n_str, tks, ROW_TILE), jnp.float32),
                        pltpu.VMEM((n_str, 1, ROW_TILE), jnp.float32),
                        pltpu.VMEM((n_str, 1, ROW_TILE), jnp.float32),
                        pltpu.VMEM((n_str, dv, ROW_TILE), jnp.float32)],
        compiler_params=_cparams(("arbitrary", "arbitrary")),
        name="flash_attention",
    )(qt, k, vt)


def _gla_block(q_ref, k_ref, v_ref, c_ref, o_ref, s_ref, ge_ref, bd_ref, r0, forward):
    nb = GLA_BLOCK
    rows = pl.ds(r0, nb)
    q, k, v, b = q_ref[rows, :], k_ref[rows, :], v_ref[rows, :], c_ref[rows, :]
    total = b[nb - 1:nb, :] if forward else b[0:1, :]
    n_pair = s_ref.shape[0]
    wk, wv = 2 * GLA_DK, 2 * GLA_DV
    qe = (q * jnp.exp(b)).astype(jnp.bfloat16)
    o = jnp.concatenate([_dot_nt(qe[:, i * wk:(i + 1) * wk], s_ref[i].astype(jnp.bfloat16))
                         for i in range(n_pair)], axis=1)
    rid = lax.broadcasted_iota(jnp.int32, b.shape, 0)
    terms = []
    for j in range(nb):
        keep = (rid >= j) if forward else (rid <= j)
        w = jnp.exp(jnp.where(keep, b - b[j:j + 1, :], -jnp.inf))
        terms.append((q * w * k[j:j + 1, :]).astype(jnp.bfloat16))
    scores = _dot(jnp.concatenate(terms, axis=0), ge_ref[...])
    for j in range(nb):
        o = o + scores[j * nb:(j + 1) * nb, :] * v[j:j + 1, :]
    o_ref[rows, :] = o
    ke = (k * jnp.exp(total - b)).astype(jnp.bfloat16)
    vb = v.astype(jnp.bfloat16)
    decay = jnp.exp(total)
    for i in range(n_pair):
        upd = lax.dot_general(vb[:, i * wv:(i + 1) * wv], ke[:, i * wk:(i + 1) * wk],
                              (((0,), (0,)), ((), ())), preferred_element_type=jnp.float32)
        s_ref[i] = decay[:, i * wk:(i + 1) * wk] * s_ref[i] + bd_ref[...] * upd


def _gla_kernel(qf, kf, vf, mf, qb, kb, vb, mb, wgf, bgf, wgb, bgb, tri_f, tri_b, ge, bd,
                of, ob, cf_ref, cb_ref, sf_ref, sb_ref):
    @pl.when(pl.program_id(0) == 0)
    def _():
        sf_ref[...] = jnp.zeros(sf_ref.shape, jnp.float32)
        sb_ref[...] = jnp.zeros(sb_ref.shape, jnp.float32)

    def cum_log_decay(m_ref, w_ref, b_ref, tri_ref):
        z = _dot_split(m_ref[...], w_ref[...]) + b_ref[...]
        la = (jnp.minimum(z, 0.0) - jnp.log1p(jnp.exp(-jnp.abs(z)))) * (1.0 / GLA_GATE_NORM)
        la_hi, la_lo = _split_bf16(la)
        tri = tri_ref[...]
        return _dot(tri, la_hi) + _dot(tri, la_lo)

    cf_ref[...] = cum_log_decay(mf, wgf, bgf, tri_f)
    cb_ref[...] = cum_log_decay(mb, wgb, bgb, tri_b)
    n_blocks = qf.shape[0] // GLA_BLOCK

    def body(t, carry):
        r_f = pl.multiple_of(t * GLA_BLOCK, GLA_BLOCK)
        r_b = pl.multiple_of((n_blocks - 1 - t) * GLA_BLOCK, GLA_BLOCK)
        _gla_block(qf, kf, vf, cf_ref, of, sf_ref, ge, bd, r_f, True)
        _gla_block(qb, kb, vb, cb_ref, ob, sb_ref, ge, bd, r_b, False)
        return carry

    lax.fori_loop(0, n_blocks, body, 0)


def gla_scan(p, w_gf_p, b_gf, w_gb_p, b_gb, consts):
    N = p.shape[0]
    tm = ROW_TILE
    nt = N // tm
    nk, nv = GLA_HEADS * GLA_DK, GLA_HEADS * GLA_DV
    fwd = lambda i: (i + nt - 1) % nt
    bwd = lambda i: (2 * nt - 1 - i) % nt
    col = lambda w, c, order: pl.BlockSpec((tm, w), lambda i: (order(i), c))
    full = lambda a: pl.BlockSpec(a.shape, lambda i: (0,) * a.ndim)
    side = lambda order: [col(nk, COL_LQ // nk, order), col(nk, COL_LK // nk, order),
                          col(nv, COL_LV // nv, order), col(LANES, COL_MISC // LANES, order)]
    small = (w_gf_p, b_gf, w_gb_p, b_gb) + tuple(consts)
    return pl.pallas_call(
        _gla_kernel,
        grid=(nt,),
        in_specs=side(fwd) + side(bwd) + [full(a) for a in small],
        out_specs=[pl.BlockSpec((tm, nv), lambda i: (fwd(i), 0)),
                   pl.BlockSpec((tm, nv), lambda i: (bwd(i), 0))],
        out_shape=[jax.ShapeDtypeStruct((N, nv), jnp.float32)] * 2,
        scratch_shapes=[pltpu.VMEM((tm, nk), jnp.float32), pltpu.VMEM((tm, nk), jnp.float32),
                        pltpu.VMEM((GLA_HEADS // 2, 2 * GLA_DV, 2 * GLA_DK), jnp.float32),
                        pltpu.VMEM((GLA_HEADS // 2, 2 * GLA_DV, 2 * GLA_DK), jnp.float32)],
        compiler_params=_cparams(("arbitrary",)),
        name="gla_scan",
    )(p, p, p, p, p, p, p, p, *small)


def _gla_consts():
    i = jnp.arange(ROW_TILE)
    same = (i[:, None] // GLA_BLOCK) == (i[None, :] // GLA_BLOCK)
    tri_f = (same & (i[None, :] <= i[:, None])).astype(jnp.bfloat16)
    tri_b = (same & (i[None, :] >= i[:, None])).astype(jnp.bfloat16)
    hk = jnp.arange(GLA_HEADS * GLA_DK) // GLA_DK
    hv = jnp.arange(GLA_HEADS * GLA_DV) // GLA_DV
    ge = (hk[:, None] == hv[None, :]).astype(jnp.bfloat16)
    bd = (hv[:2 * GLA_DV, None] == hk[None, :2 * GLA_DK]).astype(jnp.float32)
    return tri_f, tri_b, ge, bd


def _prep_gate(w_f, b_f, w_b, b_b):
    z = jnp.zeros((LANES, w_f.shape[1]), jnp.float32)
    return (z.at[MLA_ROPE:MLA_ROPE + GLA_GATE_RANK].set(w_f), b_f[None],
            z.at[MLA_ROPE + GLA_GATE_RANK:MLA_ROPE + 2 * GLA_GATE_RANK].set(w_b), b_b[None])


def _layer_norm(u, g, b):
    mu = jnp.mean(u, axis=-1, keepdims=True)
    d = u - mu
    var = jnp.mean(d * d, axis=-1, keepdims=True)
    return d * lax.rsqrt(var + LN_EPS) * g + b


def _first_max(rows):
    best, idx = rows[0], jnp.zeros(rows[0].shape, jnp.int32)
    for i in range(1, len(rows)):
        better = rows[i] > best
        idx = jnp.where(better, i, idx)
        best = jnp.where(better, rows[i], best)
    return best, idx


def _route(lt):
    grp_rows = [lt[g:g + 1, :] for g in range(N_GROUPS)]
    g_best, grp = _first_max(grp_rows)
    grp_w = 1.0 / sum(jnp.exp(r - g_best) for r in grp_rows)
    in_grp = []
    for j in range(EXPERTS_PER_GROUP):
        val = lt[8 + j:9 + j, :]
        for g in range(1, N_GROUPS):
            r = 8 + g * EXPERTS_PER_GROUP + j
            val = jnp.where(grp == g, lt[r:r + 1, :], val)
        in_grp.append(val)
    v1, i1 = _first_max(in_grp)
    rest = [jnp.where(i1 == j, -jnp.inf, in_grp[j]) for j in range(EXPERTS_PER_GROUP)]
    v2, i2 = _first_max(rest)
    e = jnp.exp(v2 - v1)
    w1 = 1.0 / (1.0 + e)
    ids = jnp.concatenate([grp * EXPERTS_PER_GROUP + i1, grp * EXPERTS_PER_GROUP + i2], axis=0)
    gates = jnp.concatenate([grp_w * w1, grp_w * (e * w1)], axis=0)
    return ids, gates


def _outproj_kernel(n_lat_tiles, alpha,
                    x_ref, yml, ymc, ygl, ygc, of_ref, ob_ref, g_ref, on_ref, wo_ref,
                    g1_ref, sh2_ref, sc2_ref, lg_ref, lb_ref, wr_ref, br_ref,
                    x1_ref, h2_ref, id_ref, gt_ref):
    is_ctx = pl.program_id(0) >= n_lat_tiles
    y_mla = jnp.where(is_ctx, ymc[...], yml[...])
    y_gqa = jnp.where(is_ctx, ygc[...], ygl[...])
    o = of_ref[...] + ob_ref[...]
    heads = [_rms(o[:, h * GLA_DV:(h + 1) * GLA_DV], on_ref[...]) for h in range(GLA_HEADS)]
    g = g_ref[...]
    y_gla = (jnp.concatenate(heads, axis=1) * (g * jax.nn.sigmoid(g))).astype(jnp.bfloat16)
    w_mla, w_gqa = y_mla.shape[1], y_gqa.shape[1]
    y = (_dot(y_mla, wo_ref[0:w_mla, :]) + _dot(y_gqa, wo_ref[w_mla:w_mla + w_gqa, :])
         + _dot(y_gla, wo_ref[w_mla + w_gqa:, :]))
    x1 = _layer_norm(alpha * x_ref[...] + _mod_row(g1_ref, is_ctx) * y, lg_ref[...], lb_ref[...])
    x1_ref[...] = x1
    h2 = x1 * (1.0 + _mod_row(sc2_ref, is_ctx)) + _mod_row(sh2_ref, is_ctx)
    h2_ref[...] = h2
    lt = _dot_split(wr_ref[...], h2, _dot_nt) + br_ref[...]
    ids, gates = _route(lt)
    rid = lax.broadcasted_iota(jnp.int32, id_ref.shape, 0)
    id_ref[...] = jnp.where(rid == 0, ids[0:1], jnp.where(rid == 1, ids[1:2], 0))
    gt_ref[...] = jnp.where(rid == 0, gates[0:1], jnp.where(rid == 1, gates[1:2], 0.0))


def out_projection(xs, y_mla, y_gqa, o_f, o_b, p, out_norm, w_out, mod_l, ln_g, ln_b, w_rt, b_rt,
                   n_lat, alpha):
    N, D = xs.shape
    tm = ROW_TILE
    nl = n_lat // tm
    assert N - n_lat == tm
    nv = GLA_HEADS * GLA_DV
    lat = lambda a: pl.BlockSpec((tm, a.shape[1]), lambda i: (jnp.minimum(i, nl - 1), 0))
    ctx = lambda a: pl.BlockSpec((tm, a.shape[1]), lambda i: (0, 0))
    row = lambda w, c: pl.BlockSpec((tm, w), lambda i: (i, c))
    full = lambda a: pl.BlockSpec(a.shape, lambda i: (0,) * a.ndim)
    modc = lambda c: pl.BlockSpec((8, D), lambda i: (0, c))
    (yml, ymc), (ygl, ygc) = y_mla, y_gqa
    return pl.pallas_call(
        functools.partial(_outproj_kernel, nl, alpha),
        grid=(N // tm,),
        in_specs=[row(D, 0), lat(yml), ctx(ymc), lat(ygl), ctx(ygc), row(nv, 0), row(nv, 0),
                  row(nv, COL_LG // nv), full(out_norm), full(w_out),
                  modc(2), modc(3), modc(4), full(ln_g), full(ln_b), full(w_rt), full(b_rt)],
        out_specs=[row(D, 0), row(D, 0), pl.BlockSpec((8, tm), lambda i: (0, i)),
                   pl.BlockSpec((8, tm), lambda i: (0, i))],
        out_shape=[jax.ShapeDtypeStruct((N, D), jnp.float32), jax.ShapeDtypeStruct((N, D), jnp.float32),
                   jax.ShapeDtypeStruct((8, N), jnp.int32), jax.ShapeDtypeStruct((8, N), jnp.float32)],
        compiler_params=_cparams(("arbitrary",)),
        name="out_projection",
    )(xs, yml, ymc, ygl, ygc, o_f, o_b, p, out_norm, w_out, mod_l, mod_l, mod_l, ln_g, ln_b, w_rt, b_rt)


def _row_copy(src_hbm, idx_ref, dst, sem, r):
    return pltpu.make_async_copy(src_hbm.at[pl.ds(idx_ref[0, 0, r], 1), :], dst.at[pl.ds(r, 1), :], sem)


def _start_rows(src_hbm, idx_ref, dst, sem, unrolled):
    n = dst.shape[0]
    if unrolled:
        for r in range(n):
            _row_copy(src_hbm, idx_ref, dst, sem, r).start()
    else:
        def start(r, c):
            _row_copy(src_hbm, idx_ref, dst, sem, r).start()
            return c
        lax.fori_loop(0, n, start, 0, unroll=8)


def _wait_rows(src_hbm, dst, sem):
    pltpu.make_async_copy(src_hbm.at[pl.ds(0, dst.shape[0]), :], dst, sem).wait()


def _double_buffered_gather(i, n_steps, src_hbm, idx_ref, idx_next_ref, bufs, sems, compute):
    @pl.when(i == 0)
    def _():
        _start_rows(src_hbm, idx_ref, bufs[0], sems.at[0], unrolled=False)

    for parity in (0, 1):
        cur, nxt = bufs[parity], bufs[1 - parity]
        for has_next in (True, False):
            more = i + 1 < n_steps
            cond = (i < n_steps) & (i % 2 == parity) & (more if has_next else jnp.logical_not(more))

            @pl.when(cond)
            def _(cur=cur, nxt=nxt, parity=parity, has_next=has_next):
                _wait_rows(src_hbm, cur, sems.at[parity])
                if has_next:
                    _start_rows(src_hbm, idx_next_ref, nxt, sems.at[1 - parity], unrolled=True)
                compute(cur)


def _expert_kernel(te_ref, nu_ref, idx_ref, idx_next_ref, h_hbm, wg_ref, wu_ref, wd_ref, y_ref,
                   xa, xb, wg_bf, wu_bf, wd_bf, sems):
    i = pl.program_id(0)
    n_used = nu_ref[0]

    @pl.when((i < n_used) & ((i == 0) | (te_ref[i] != te_ref[jnp.maximum(i - 1, 0)])))
    def _():
        wg_bf[...] = wg_ref[0, 0].astype(jnp.bfloat16)
        wu_bf[...] = wu_ref[0, 0].astype(jnp.bfloat16)
        wd_bf[...] = wd_ref[0, 0].astype(jnp.bfloat16)

    def compute(x_ref):
        x = x_ref[...].astype(jnp.bfloat16)
        a = _dot(x, wg_bf[...])
        u = _dot(x, wu_bf[...])
        hid = (a * jax.nn.sigmoid(a) * u).astype(jnp.bfloat16)
        y_ref[...] = _dot(hid, wd_bf[...])

    _double_buffered_gather(i, n_used, h_hbm, idx_ref, idx_next_ref, (xa, xb), sems, compute)

    @pl.when(i >= n_used)
    def _():
        y_ref[...] = jnp.zeros(y_ref.shape, jnp.float32)


def expert_ffn(h2, tile_expert, n_used, sorted_tok, w_g, w_u, w_d, layer):
    N, D = h2.shape
    tm = MOE_TILE
    n_tiles = sorted_tok.shape[0] // tm
    F = w_g.shape[3]
    bf = jnp.bfloat16
    idx = sorted_tok.reshape(n_tiles, 1, tm)
    nxt = lambda i, te, nu: (jnp.minimum(i + 1, n_tiles - 1), 0, 0)
    gs = pltpu.PrefetchScalarGridSpec(
        num_scalar_prefetch=2,
        grid=(n_tiles,),
        in_specs=[pl.BlockSpec((1, 1, tm), lambda i, te, nu: (i, 0, 0), memory_space=pltpu.SMEM),
                  pl.BlockSpec((1, 1, tm), nxt, memory_space=pltpu.SMEM),
                  pl.BlockSpec(memory_space=pl.ANY),
                  pl.BlockSpec((1, 1, D, F), lambda i, te, nu: (layer, te[i], 0, 0)),
                  pl.BlockSpec((1, 1, D, F), lambda i, te, nu: (layer, te[i], 0, 0)),
                  pl.BlockSpec((1, 1, F, D), lambda i, te, nu: (layer, te[i], 0, 0))],
        out_specs=pl.BlockSpec((tm, D), lambda i, te, nu: (i, 0)),
        scratch_shapes=[pltpu.VMEM((tm, D), jnp.float32), pltpu.VMEM((tm, D), jnp.float32),
                        pltpu.VMEM((D, F), bf), pltpu.VMEM((D, F), bf), pltpu.VMEM((F, D), bf),
                        pltpu.SemaphoreType.DMA((2,))],
    )
    return pl.pallas_call(
        _expert_kernel,
        grid_spec=gs,
        out_shape=jax.ShapeDtypeStruct((n_tiles * tm, D), jnp.float32),
        compiler_params=_cparams(("arbitrary",)),
        name="expert_ffn",
    )(tile_expert, n_used, idx, idx, h2, w_g, w_u, w_d)


def _combine_kernel(n_lat_tiles, alpha, pos_ref, pos_next_ref, y_hbm, x_ref, gc_ref, g2_ref, lg_ref,
                    lb_ref, o_ref, ya, yb, sems):
    i = pl.program_id(0)
    is_ctx = i >= n_lat_tiles
    tm = x_ref.shape[0]

    def compute(y_ref):
        gc = gc_ref[...]
        f = gc[:, 0:1] * y_ref[0:tm, :] + gc[:, 1:2] * y_ref[tm:2 * tm, :]
        o_ref[...] = _layer_norm(alpha * x_ref[...] + _mod_row(g2_ref, is_ctx) * f, lg_ref[...], lb_ref[...])

    _double_buffered_gather(i, pl.num_programs(0), y_hbm, pos_ref, pos_next_ref, (ya, yb), sems, compute)


def moe_combine(xs1, y_sorted, pos, gate_cols, mod_l, ln_g, ln_b, n_lat, alpha):
    N, D = xs1.shape
    tm = ROW_TILE
    nt = N // tm
    full = lambda a: pl.BlockSpec(a.shape, lambda i: (0,) * a.ndim)
    return pl.pallas_call(
        functools.partial(_combine_kernel, n_lat // tm, alpha),
        grid=(nt,),
        in_specs=[pl.BlockSpec((1, 1, 2 * tm), lambda i: (i, 0, 0), memory_space=pltpu.SMEM),
                  pl.BlockSpec((1, 1, 2 * tm), lambda i: (jnp.minimum(i + 1, nt - 1), 0, 0),
                               memory_space=pltpu.SMEM),
                  pl.BlockSpec(memory_space=pl.ANY),
                  pl.BlockSpec((tm, D), lambda i: (i, 0)),
                  pl.BlockSpec((tm, 2), lambda i: (i, 0)),
                  pl.BlockSpec((8, D), lambda i: (0, 5)), full(ln_g), full(ln_b)],
        out_specs=pl.BlockSpec((tm, D), lambda i: (i, 0)),
        out_shape=jax.ShapeDtypeStruct((N, D), jnp.float32),
        scratch_shapes=[pltpu.VMEM((2 * tm, D), jnp.float32), pltpu.VMEM((2 * tm, D), jnp.float32),
                        pltpu.SemaphoreType.DMA((2,))],
        compiler_params=_cparams(("arbitrary",)),
        name="moe_combine",
    )(pos, pos, y_sorted, xs1, gate_cols, mod_l, ln_g, ln_b)


def _dispatch_plan(ids, n_tok):
    tm = MOE_TILE
    n_asg = 2 * n_tok
    assert n_asg * n_asg < 2 ** 31
    n_tiles = n_asg // tm + N_EXPERTS
    i32 = jnp.int32
    e_flat = ids.reshape(n_asg)
    ar = jnp.arange(n_asg, dtype=i32)
    order = jnp.sort(e_flat * n_asg + ar) % n_asg
    rank = jnp.sort(order * n_asg + ar) % n_asg
    experts = jnp.arange(N_EXPERTS, dtype=i32)
    onehot = e_flat[:, None] == experts[None, :]
    counts = jnp.sum(onehot, axis=0, dtype=i32)
    ends = jnp.cumsum(counts).astype(i32)
    padded = ((counts + tm - 1) // tm) * tm
    pad_ends = jnp.cumsum(padded).astype(i32)
    shift = (pad_ends - padded) - (ends - counts)
    pos = rank + jnp.sum(jnp.where(onehot, shift[None, :], 0), axis=1, dtype=i32)
    n_used = (pad_ends[-1] // tm).reshape(1)
    tile_start = jnp.arange(n_tiles, dtype=i32) * tm
    tile_expert = jnp.sum(tile_start[:, None] >= pad_ends[None, :], axis=1, dtype=i32)
    last = jnp.sum((pad_ends[-1] - 1) >= pad_ends, dtype=i32)
    tile_expert = jnp.minimum(tile_expert, last)
    of_tile = tile_expert[:, None] == experts[None, :]
    shift_t = jnp.sum(jnp.where(of_tile, shift[None, :], 0), axis=1, dtype=i32)
    ends_t = jnp.sum(jnp.where(of_tile, ends[None, :], 0), axis=1, dtype=i32)
    r = tile_start[:, None] + jnp.arange(tm, dtype=i32)[None, :] - shift_t[:, None]
    valid = (r < ends_t[:, None]) & (tile_start[:, None] < pad_ends[-1])
    sorted_tok = jnp.where(valid, order[jnp.clip(r, 0, n_asg - 1)] % n_tok, 0).reshape(n_tiles * tm)
    nt = n_tok // ROW_TILE
    pos_tiles = jnp.concatenate([pos[:n_tok].reshape(nt, 1, ROW_TILE),
                                 pos[n_tok:].reshape(nt, 1, ROW_TILE)], axis=2)
    return sorted_tok, tile_expert, n_used, pos_tiles


def _rope_tables(n_lat, n_ctx, dim):
    rows = n_lat // GRID_W
    row = jnp.repeat(jnp.arange(rows), GRID_W).astype(jnp.float32)
    col = jnp.tile(jnp.arange(GRID_W), rows).astype(jnp.float32)
    half = dim // 2
    inv_freq = ROPE_THETA ** (-jnp.arange(0, half, 2, dtype=jnp.float32) / half)
    ang_r = row[:, None] * inv_freq
    ang_c = col[:, None] * inv_freq
    ang = jnp.concatenate([ang_r, ang_r, ang_c, ang_c], axis=-1)
    cos, sin = jnp.cos(ang), jnp.sin(ang)
    q = dim // 4
    lane = jnp.arange(dim)
    first = ((lane // q) % 2) == 0
    sin_a = jnp.where(first, -sin, 0.0)
    sin_b = jnp.where(first, 0.0, sin)

    def pad(t, fill):
        t = jnp.pad(t, ((0, 0), (0, LANES - dim)), constant_values=fill)
        return jnp.pad(t, ((0, n_ctx), (0, 0)), constant_values=fill)

    return pad(cos, 1.0), pad(sin_a, 0.0), pad(sin_b, 0.0)


def _in_perm():
    a = MLA_Q_RANK + MLA_KV_RANK + MLA_ROPE
    b = a + (GQA_HEADS + 2 * GQA_KV_HEADS) * HEAD_DIM
    nk, nv = GLA_HEADS * GLA_DK, GLA_HEADS * GLA_DV
    r = lambda s, n: list(range(s, s + n))
    cols = (r(0, MLA_Q_RANK + MLA_KV_RANK)
            + r(a, b - a)
            + r(b, 2 * nk + 2 * nv)
            + r(MLA_Q_RANK + MLA_KV_RANK, MLA_ROPE)
            + r(b + 2 * nk + 2 * nv, 2 * GLA_GATE_RANK))
    return jnp.asarray(cols, jnp.int32)


def _prep_w_in(w_in_l):
    w = w_in_l[:, _in_perm()]
    scale = jnp.ones((w.shape[1],), jnp.float32).at[COL_LQ:COL_LK].set(GLA_DK ** -0.5)
    w = w * scale
    return jnp.pad(w, ((0, 0), (0, P_COLS - w.shape[1]))).astype(jnp.bfloat16)


def _prep_w_uq(w_uq_l):
    r = w_uq_l.shape[0]
    w = w_uq_l.reshape(r, MLA_HEADS, MLA_NOPE + MLA_ROPE)
    w = jnp.pad(w, ((0, 0), (0, 0), (0, MLA_QPAD - MLA_NOPE - MLA_ROPE)))
    return w.reshape(r, MLA_HEADS * MLA_QPAD).astype(jnp.bfloat16)


def _prep_router(w_rg, b_rg, w_re, b_re):
    D = w_rg.shape[0]
    w = jnp.zeros((LANES, D), jnp.float32).at[0:N_GROUPS].set(w_rg.T).at[8:8 + N_EXPERTS].set(w_re.T)
    b = jnp.zeros((LANES,), jnp.float32).at[0:N_GROUPS].set(b_rg).at[8:8 + N_EXPERTS].set(b_re)
    return w, b[:, None]


def _q_split(n_lat, want_tiles):
    tiles = n_lat // ROW_TILE
    q_tiles = want_tiles if tiles % want_tiles == 0 else 1
    q_groups = max(g for g in (4, 2, 1) if (tiles // q_tiles) % g == 0)
    return dict(q_tiles=q_tiles, q_groups=q_groups)


def _kv_tile(n):
    return max(t for t in (256, 640, 1280) if n % t == 0)


def kernel(x, c, ctx, c_ctx, ada_w, ada_b, w_in, mla_q_norm, mla_w_uq, mla_kv_norm, mla_w_ukv, gqa_q_norm, gqa_k_norm, gla_w_gate_fwd, gla_b_gate_fwd, gla_w_gate_bwd, gla_b_gate_bwd, gla_out_norm, w_out, ln1_g, ln1_b, w_route_group, b_route_group, w_route_expert, b_route_expert, w_expert_gate, w_expert_up, w_expert_down, ln2_g, ln2_b):
    T, C = x.shape[1], ctx.shape[1]
    N = T + C
    L = ada_w.shape[0]
    assert x.shape[0] == 1 and C == ROW_TILE and T % ROW_TILE == 0 and T % GRID_W == 0
    alpha = (2.0 * L) ** 0.25
    bf = jnp.bfloat16

    xs = jnp.concatenate([x[0], ctx[0]], axis=0)
    mods = modulation(c, c_ctx, ada_w, ada_b)
    tabs_mla = _rope_tables(T, C, MLA_ROPE)
    tabs_gqa = _rope_tables(T, C, HEAD_DIM)
    gla_consts = _gla_consts()
    mla_scale = (MLA_NOPE + MLA_ROPE) ** -0.5 * LOG2E
    gqa_scale = HEAD_DIM ** -0.5 * LOG2E
    tabs_mla_t = tuple(t.T for t in tabs_mla)
    tabs_gqa_t = tuple(t.T for t in tabs_gqa)
    lat = dict(q_row0=0, n_q=T, k_row0=0, n_k=N, tks=_kv_tile(N))
    con = dict(q_row0=T, n_q=C, k_row0=T, n_k=C, tks=C)

    for l in range(L):
        mod_l = mods[l]
        p = in_projection(xs, mod_l, _prep_w_in(w_in[l]), T)

        w_ukv = mla_w_ukv[l].reshape(MLA_KV_RANK, MLA_HEADS, MLA_NOPE + MLA_V)
        w_kn = w_ukv[:, :, :MLA_NOPE].reshape(MLA_KV_RANK, MLA_HEADS * MLA_NOPE).astype(bf)
        w_v_t = w_ukv[:, :, MLA_NOPE:].reshape(MLA_KV_RANK, MLA_HEADS * MLA_V).T.astype(bf)
        qt, k, vt = mla_prep(p, (mla_q_norm[l] * mla_scale)[None], _prep_w_uq(mla_w_uq[l]).T,
                             mla_kv_norm[l][None], w_kn, w_v_t, tabs_mla, tabs_mla_t)
        y_mla = (flash_attention(qt, k, vt, group=1, **_q_split(T, 4), **lat),
                 flash_attention(qt, k, vt, group=1, q_tiles=1, q_groups=1, **con))
        qt, k, vt = gqa_prep(p, (gqa_q_norm[l] * gqa_scale)[:, None], gqa_k_norm[l][None],
                             tabs_gqa, tabs_gqa_t)
        y_gqa = (flash_attention(qt, k, vt, group=GQA_GROUP, **_q_split(T, 1), **lat),
                 flash_attention(qt, k, vt, group=GQA_GROUP, q_tiles=1, q_groups=1, **con))
        o_f, o_b = gla_scan(p, *_prep_gate(gla_w_gate_fwd[l], gla_b_gate_fwd[l],
                                           gla_w_gate_bwd[l], gla_b_gate_bwd[l]), gla_consts)

        w_rt, b_rt = _prep_router(w_route_group[l], b_route_group[l], w_route_expert[l], b_route_expert[l])
        x1, h2, ids, gates = out_projection(xs, y_mla, y_gqa, o_f, o_b, p, gla_out_norm[l][None],
                                            w_out[l].astype(bf), mod_l, ln1_g[l][None], ln1_b[l][None],
                                            w_rt, b_rt, T, alpha)

        sorted_tok, tile_expert, n_used, pos_tiles = _dispatch_plan(ids[0:2], N)
        y_sorted = expert_ffn(h2, tile_expert, n_used, sorted_tok, w_expert_gate, w_expert_up,
                              w_expert_down, l)
        xs = moe_combine(x1, y_sorted, pos_tiles, gates[0:2].T, mod_l, ln2_g[l][None], ln2_b[l][None],
                         T, alpha)
    return xs[:T][None]
```

```python
import functools
import math

import jax
import jax.numpy as jnp
from jax import lax
from jax.experimental import pallas as pl
from jax.experimental.pallas import tpu as pltpu

GRID_W = 64
ROPE_THETA = 10000.0
NORM_EPS = 1e-6
LN_EPS = 1e-5
HEAD_DIM = 128
MLA_HEADS = 6
MLA_Q_RANK = 512
MLA_KV_RANK = 256
MLA_NOPE = 128
MLA_ROPE = 64
MLA_V = 128
GQA_HEADS = 6
GQA_KV_HEADS = 2
GQA_GROUP = GQA_HEADS // GQA_KV_HEADS
GLA_HEADS = 4
GLA_DK = 64
GLA_DV = 128
GLA_GATE_RANK = 16
GLA_GATE_NORM = 16.0
N_GROUPS = 4
EXPERTS_PER_GROUP = 8
N_EXPERTS = N_GROUPS * EXPERTS_PER_GROUP
N_MOD = 6

LANES = 128
VMEM_LIMIT = 48 * 1024 * 1024

ROW_TILE = 256
GLA_BLOCK = 16
MOE_TILE = 256

COL_CQ = 0
COL_CKV = 512
COL_GQ = 768
COL_GK = 1536
COL_GV = 1792
COL_LQ = 2048
COL_LK = 2304
COL_LV = 2560
COL_LG = 3072
COL_MISC = 3584
P_COLS = 3840
P_HALF = P_COLS // 2
MLA_QPAD = 256

LOG2E = math.log2(math.e)


def _cparams(sem):
    return pltpu.CompilerParams(dimension_semantics=sem, vmem_limit_bytes=VMEM_LIMIT)


def _dot(a, b):
    return jnp.dot(a, b, preferred_element_type=jnp.float32)


def _dot_nt(a, b):
    return lax.dot_general(a, b, (((1,), (1,)), ((), ())), preferred_element_type=jnp.float32)


def _split_bf16(x):
    hi = x.astype(jnp.bfloat16)
    return hi, (x - hi.astype(jnp.float32)).astype(jnp.bfloat16)


def _dot_split(a, b, dot=_dot):
    a_hi, a_lo = _split_bf16(a)
    b_hi, b_lo = _split_bf16(b)
    return dot(a_hi, b_hi) + dot(a_lo, b_hi) + dot(a_hi, b_lo)


def _mod_kernel(crep_ref, w_ref, b_ref, o_ref):
    tn = w_ref.shape[2]
    rows = []
    for r in range(2):
        cr = crep_ref[r]
        a = cr * jax.nn.sigmoid(cr)
        parts = []
        for j in range(tn // LANES):
            wj = w_ref[0, :, j * LANES:(j + 1) * LANES]
            parts.append(jnp.sum(wj * a, axis=0, keepdims=True))
        rows.append(jnp.concatenate(parts, axis=1) + b_ref[0])
    rid = lax.broadcasted_iota(jnp.int32, (8, tn), 0)
    o_ref[0] = jnp.where(rid == 0, rows[0], jnp.where(rid == 1, rows[1], 0.0))


def modulation(c, c_ctx, ada_w, ada_b):
    L, D, M = ada_w.shape
    tn = max(t for t in range(LANES, 1024 + 1, LANES) if M % t == 0)
    crep = jnp.broadcast_to(jnp.stack([c[0], c_ctx])[:, :, None], (2, D, LANES))
    return pl.pallas_call(
        _mod_kernel,
        grid=(L, M // tn),
        in_specs=[pl.BlockSpec((2, D, LANES), lambda l, j: (0, 0, 0)),
                  pl.BlockSpec((1, D, tn), lambda l, j: (l, 0, j)),
                  pl.BlockSpec((1, 1, tn), lambda l, j: (l, 0, j))],
        out_specs=pl.BlockSpec((1, 8, tn), lambda l, j: (l, 0, j)),
        out_shape=jax.ShapeDtypeStruct((L, 8, M), jnp.float32),
        compiler_params=_cparams(("arbitrary", "arbitrary")),
        name="modulation",
    )(crep, ada_w, ada_b.reshape(L, 1, M))


def _mod_row(ref, is_ctx):
    return jnp.where(is_ctx, ref[1:2, :], ref[0:1, :])


def _inproj_kernel(n_lat_tiles, x_ref, sh_ref, sc_ref, w_ref, o_ref):
    is_ctx = pl.program_id(1) >= n_lat_tiles
    h = x_ref[...] * (1.0 + _mod_row(sc_ref, is_ctx)) + _mod_row(sh_ref, is_ctx)
    o_ref[...] = _dot(h.astype(jnp.bfloat16), w_ref[...])


def in_projection(xs, mod_l, w_in_p, n_lat):
    N, D = xs.shape
    tm = ROW_TILE
    return pl.pallas_call(
        functools.partial(_inproj_kernel, n_lat // tm),
        grid=(2, N // tm),
        in_specs=[pl.BlockSpec((tm, D), lambda j, i: (i, 0)),
                  pl.BlockSpec((8, D), lambda j, i: (0, 0)),
                  pl.BlockSpec((8, D), lambda j, i: (0, 1)),
                  pl.BlockSpec((D, P_HALF), lambda j, i: (0, j))],
        out_specs=pl.BlockSpec((tm, P_HALF), lambda j, i: (i, j)),
        out_shape=jax.ShapeDtypeStruct((N, P_COLS), jnp.float32),
        compiler_params=_cparams(("arbitrary", "arbitrary")),
        name="in_projection",
    )(xs, mod_l, mod_l, w_in_p)


def _rope(x, cos, sin_a, sin_b, quarter):
    w = x.shape[-1]
    return x * cos + pltpu.roll(x, w - quarter, 1) * sin_a + pltpu.roll(x, quarter, 1) * sin_b


def _rms(x, gain):
    return x * lax.rsqrt(jnp.mean(x * x, axis=-1, keepdims=True) + NORM_EPS) * gain


def _rope_t(x, cos, sin_a, sin_b, quarter):
    w = x.shape[0]
    return x * cos + pltpu.roll(x, w - quarter, 0) * sin_a + pltpu.roll(x, quarter, 0) * sin_b


def _mla_prep_kernel(cq_ref, ckv_ref, misc_ref, qn_ref, wuqt_ref, kvn_ref, wkn_ref, wvt_ref,
                     cos_ref, sa_ref, sb_ref, cost_ref, sat_ref, sbt_ref, qt_ref, k_ref, vt_ref):
    cq = _rms(cq_ref[...], qn_ref[...]).astype(jnp.bfloat16)
    qt_all = _dot_nt(wuqt_ref[...], cq)
    ckv = _rms(ckv_ref[...], kvn_ref[...]).astype(jnp.bfloat16)
    kn_all = _dot(ckv, wkn_ref[...])
    vt_all = _dot_nt(wvt_ref[...], ckv)
    lane = lax.broadcasted_iota(jnp.int32, misc_ref.shape, 1)
    k_rope = _rope(misc_ref[...], cos_ref[...], sa_ref[...], sb_ref[...], MLA_ROPE // 4)
    k_rope = jnp.where(lane < MLA_ROPE, k_rope, 0.0).astype(jnp.bfloat16)
    cost, sat, sbt = cost_ref[...], sat_ref[...], sbt_ref[...]
    for h in range(MLA_HEADS):
        b0 = h * MLA_QPAD
        qt_ref[h, 0:LANES, :] = qt_all[b0:b0 + LANES, :].astype(jnp.bfloat16)
        qt_ref[h, LANES:2 * LANES, :] = _rope_t(qt_all[b0 + LANES:b0 + 2 * LANES, :], cost, sat, sbt,
                                                MLA_ROPE // 4).astype(jnp.bfloat16)
        k_ref[h, :, 0:LANES] = kn_all[:, h * MLA_NOPE:(h + 1) * MLA_NOPE].astype(jnp.bfloat16)
        k_ref[h, :, LANES:2 * LANES] = k_rope
        vt_ref[h] = vt_all[h * MLA_V:(h + 1) * MLA_V, :].astype(jnp.bfloat16)


def mla_prep(p, q_gain, w_uq_t, kv_gain, w_kn, w_v_t, tabs, tabs_t):
    N = p.shape[0]
    tm = ROW_TILE
    H = MLA_HEADS
    row = lambda w, c: pl.BlockSpec((tm, w), lambda i: (i, c))
    full = lambda a: pl.BlockSpec(a.shape, lambda i: (0,) * a.ndim)
    tab = pl.BlockSpec((tm, LANES), lambda i: (i, 0))
    tab_t = pl.BlockSpec((LANES, tm), lambda i: (0, i))
    return pl.pallas_call(
        _mla_prep_kernel,
        grid=(N // tm,),
        in_specs=[row(MLA_Q_RANK, COL_CQ // MLA_Q_RANK), row(MLA_KV_RANK, COL_CKV // MLA_KV_RANK),
                  row(LANES, COL_MISC // LANES),
                  full(q_gain), full(w_uq_t), full(kv_gain), full(w_kn), full(w_v_t),
                  tab, tab, tab, tab_t, tab_t, tab_t],
        out_specs=[pl.BlockSpec((H, MLA_QPAD, tm), lambda i: (0, 0, i)),
                   pl.BlockSpec((H, tm, MLA_QPAD), lambda i: (0, i, 0)),
                   pl.BlockSpec((H, MLA_V, tm), lambda i: (0, 0, i))],
        out_shape=[jax.ShapeDtypeStruct((H, MLA_QPAD, N), jnp.bfloat16),
                   jax.ShapeDtypeStruct((H, N, MLA_QPAD), jnp.bfloat16),
                   jax.ShapeDtypeStruct((H, MLA_V, N), jnp.bfloat16)],
        compiler_params=_cparams(("arbitrary",)),
        name="mla_prep",
    )(p, p, p, q_gain, w_uq_t, kv_gain, w_kn, w_v_t, *tabs, *tabs_t)


def _gqa_prep_kernel(q_in, k_in, v_in, qn_ref, kn_ref, cos_ref, sa_ref, sb_ref,
                     cost_ref, sat_ref, sbt_ref, qt_ref, k_ref, vt_ref):
    cos, sa, sb = cos_ref[...], sa_ref[...], sb_ref[...]
    cost, sat, sbt = cost_ref[...], sat_ref[...], sbt_ref[...]
    qt = HEAD_DIM // 4
    for h in range(GQA_HEADS):
        xt = q_in[:, h * HEAD_DIM:(h + 1) * HEAD_DIM].T
        xt = xt * lax.rsqrt(jnp.mean(xt * xt, axis=0, keepdims=True) + NORM_EPS) * qn_ref[...]
        qt_ref[h] = _rope_t(xt, cost, sat, sbt, qt).astype(jnp.bfloat16)
    for h in range(GQA_KV_HEADS):
        x = _rms(k_in[:, h * HEAD_DIM:(h + 1) * HEAD_DIM], kn_ref[...])
        k_ref[h] = _rope(x, cos, sa, sb, qt).astype(jnp.bfloat16)
        vt_ref[h] = v_in[:, h * HEAD_DIM:(h + 1) * HEAD_DIM].T.astype(jnp.bfloat16)


def gqa_prep(p, q_gain_col, k_gain, tabs, tabs_t):
    N = p.shape[0]
    tm = ROW_TILE
    wq, wk = GQA_HEADS * HEAD_DIM, GQA_KV_HEADS * HEAD_DIM
    row = lambda w, c: pl.BlockSpec((tm, w), lambda i: (i, c))
    full = lambda a: pl.BlockSpec(a.shape, lambda i: (0,) * a.ndim)
    tab = pl.BlockSpec((tm, LANES), lambda i: (i, 0))
    tab_t = pl.BlockSpec((LANES, tm), lambda i: (0, i))
    return pl.pallas_call(
        _gqa_prep_kernel,
        grid=(N // tm,),
        in_specs=[row(wq, COL_GQ // wq), row(wk, COL_GK // wk), row(wk, COL_GV // wk),
                  full(q_gain_col), full(k_gain), tab, tab, tab, tab_t, tab_t, tab_t],
        out_specs=[pl.BlockSpec((GQA_HEADS, HEAD_DIM, tm), lambda i: (0, 0, i)),
                   pl.BlockSpec((GQA_KV_HEADS, tm, HEAD_DIM), lambda i: (0, i, 0)),
                   pl.BlockSpec((GQA_KV_HEADS, HEAD_DIM, tm), lambda i: (0, 0, i))],
        out_shape=[jax.ShapeDtypeStruct((GQA_HEADS, HEAD_DIM, N), jnp.bfloat16),
                   jax.ShapeDtypeStruct((GQA_KV_HEADS, N, HEAD_DIM), jnp.bfloat16),
                   jax.ShapeDtypeStruct((GQA_KV_HEADS, HEAD_DIM, N), jnp.bfloat16)],
        compiler_params=_cparams(("arbitrary",)),
        name="gqa_prep",
    )(p, p, p, q_gain_col, k_gain, *tabs, *tabs_t)


KEY_CHUNK = 256
FLASH_UNROLL = 2


def _sublane_fold(x, op):
    return op(x.reshape(x.shape[0] // 8, 8, x.shape[1]), axis=0)


def _flash_kernel(group, q_tiles, q_groups, tks, qt_ref, k_ref, vt_ref, o_ref, s_even, s_odd,
                  m_ref, l_ref, acc_ref):
    dv = vt_ref.shape[1]
    tq = ROW_TILE
    n_chunks = tks // KEY_CHUNK
    n_super = k_ref.shape[1] // tks
    streams = [(g, t) for g in range(group) for t in range(q_tiles)]

    def reset_state():
        m_ref[...] = jnp.full(m_ref.shape, -jnp.inf, jnp.float32)
        l_ref[...] = jnp.zeros(l_ref.shape, jnp.float32)
        acc_ref[...] = jnp.zeros(acc_ref.shape, jnp.float32)

    def stage(qg_a, j_a, buf_a, j_b, buf_b, cmax_b):
        if j_a is not None:
            col0 = qg_a * q_tiles * tq
            ka = pl.multiple_of(j_a * tks, tks)
            cmax_a = [None] * len(streams)
        if j_b is not None:
            kb = pl.multiple_of(j_b * tks, tks)
            m_old = [m_ref[s] for s in range(len(streams))]
            m_new = [jnp.maximum(m_old[s], jnp.max(cmax_b[s], axis=0, keepdims=True))
                     for s in range(len(streams))]
            lsum = [jnp.zeros((8, tq), jnp.float32) for _ in streams]
            pv = [None] * len(streams)
        for c in range(n_chunks):
            rows = slice(c * KEY_CHUNK, (c + 1) * KEY_CHUNK)
            if j_a is not None:
                k_c = k_ref[0, pl.ds(ka + c * KEY_CHUNK, KEY_CHUNK), :]
            if j_b is not None:
                vt_c = vt_ref[0, :, pl.ds(kb + c * KEY_CHUNK, KEY_CHUNK)]
            for s, (g, t) in enumerate(streams):
                if j_a is not None:
                    sc = _dot(k_c, qt_ref[g, :, col0 + t * tq:col0 + (t + 1) * tq])
                    buf_a[s, rows, :] = sc
                    cm = _sublane_fold(sc, jnp.max)
                    cmax_a[s] = cm if cmax_a[s] is None else jnp.maximum(cmax_a[s], cm)
                if j_b is not None:
                    p = jnp.exp2(buf_b[s, rows, :] - m_new[s])
                    lsum[s] = lsum[s] + _sublane_fold(p, jnp.sum)
                    d = _dot(vt_c, p.astype(jnp.bfloat16))
                    pv[s] = d if pv[s] is None else d + pv[s]
        if j_b is not None:
            for s in range(len(streams)):
                alpha = jnp.exp2(m_old[s] - m_new[s])
                l_ref[s] = alpha * l_ref[s] + jnp.sum(lsum[s], axis=0, keepdims=True)
                acc_ref[s] = alpha * acc_ref[s] + pv[s]
                m_ref[s] = m_new[s]
        return tuple(cmax_a) if j_a is not None else None

    bufs = (s_even, s_odd)

    def stages(qg, j0, count, cm):
        base = qg * n_super
        for d in range(count):
            cm = stage(qg, j0 + d, bufs[(base + 1 + d) % 2], j0 + d - 1, bufs[(base + d) % 2], cm)
        return cm

    def finish(qg):
        row0 = qg * q_tiles * tq
        for s, (g, t) in enumerate(streams):
            o_ref[row0 + t * tq:row0 + (t + 1) * tq, g * dv:(g + 1) * dv] = (
                acc_ref[s] / l_ref[s]).T.astype(o_ref.dtype)

    n_loop = (n_super - 1) // FLASH_UNROLL
    reset_state()
    cmax = stage(0, 0, bufs[0], None, None, None)
    for qg in range(q_groups):
        cmax = lax.fori_loop(0, n_loop, lambda i, cm, qg=qg: stages(qg, FLASH_UNROLL * i + 1, FLASH_UNROLL, cm),
                             cmax)
        cmax = stages(qg, FLASH_UNROLL * n_loop + 1, (n_super - 1) % FLASH_UNROLL, cmax)
        last_buf = bufs[(qg * n_super + n_super - 1) % 2]
        if qg + 1 < q_groups:
            cmax = stage(qg + 1, 0, bufs[((qg + 1) * n_super) % 2], n_super - 1, last_buf, cmax)
            finish(qg)
            reset_state()
        else:
            stage(None, None, None, n_super - 1, last_buf, cmax)
            finish(qg)


def flash_attention(qt, k, vt, *, group, q_tiles, q_groups, tks, q_row0, n_q, k_row0, n_k):
    hk, _, dq = k.shape
    dv = vt.shape[1]
    tq = ROW_TILE * q_tiles * q_groups
    assert q_row0 % tq == 0 and n_q % tq == 0 and k_row0 % n_k == 0 and n_k % tks == 0
    assert tks % KEY_CHUNK == 0 and FLASH_UNROLL % 2 == 0
    qb, kb = q_row0 // tq, k_row0 // n_k
    n_str = group * q_tiles
    return pl.pallas_call(
        functools.partial(_flash_kernel, group, q_tiles, q_groups, tks),
        grid=(hk, n_q // tq),
        in_specs=[pl.BlockSpec((group, dq, tq), lambda h, i: (h, 0, qb + i)),
                  pl.BlockSpec((1, n_k, dq), lambda h, i: (h, kb, 0)),
                  pl.BlockSpec((1, dv, n_k), lambda h, i: (h, 0, kb))],
        out_specs=pl.BlockSpec((tq, group * dv), lambda h, i: (i, h)),
        out_shape=jax.ShapeDtypeStruct((n_q, hk * group * dv), jnp.bfloat16),
        scratch_shapes=[pltpu.VMEM((n_str, tks, ROW_TILE), jnp.float32),
                        pltpu.VMEM((n_str, tks, ROW_TILE), jnp.float32),
                        pltpu.VMEM((n_str, 1, ROW_TILE), jnp.float32),
                        pltpu.VMEM((n_str, 1, ROW_TILE), jnp.float32),
                        pltpu.VMEM((n_str, dv, ROW_TILE), jnp.float32)],
        compiler_params=_cparams(("arbitrary", "arbitrary")),
        name="flash_attention",
    )(qt, k, vt)


def _gla_block(q_ref, k_ref, v_ref, c_ref, o_ref, s_ref, ge_ref, bd_ref, r0, forward):
    nb = GLA_BLOCK
    rows = pl.ds(r0, nb)
    q, k, v, b = q_ref[rows, :], k_ref[rows, :], v_ref[rows, :], c_ref[rows, :]
    total = b[nb - 1:nb, :] if forward else b[0:1, :]
    n_pair = s_ref.shape[0]
    wk, wv = 2 * GLA_DK, 2 * GLA_DV
    qe = (q * jnp.exp(b)).astype(jnp.bfloat16)
    o = jnp.concatenate([_dot_nt(qe[:, i * wk:(i + 1) * wk], s_ref[i].astype(jnp.bfloat16))
                         for i in range(n_pair)], axis=1)
    rid = lax.broadcasted_iota(jnp.int32, b.shape, 0)
    terms = []
    for j in range(nb):
        keep = (rid >= j) if forward else (rid <= j)
        w = jnp.exp(jnp.where(keep, b - b[j:j + 1, :], -jnp.inf))
        terms.append((q * w * k[j:j + 1, :]).astype(jnp.bfloat16))
    scores = _dot(jnp.concatenate(terms, axis=0), ge_ref[...])
    for j in range(nb):
        o = o + scores[j * nb:(j + 1) * nb, :] * v[j:j + 1, :]
    o_ref[rows, :] = o
    ke = (k * jnp.exp(total - b)).astype(jnp.bfloat16)
    vb = v.astype(jnp.bfloat16)
    decay = jnp.exp(total)
    for i in range(n_pair):
        upd = lax.dot_general(vb[:, i * wv:(i + 1) * wv], ke[:, i * wk:(i + 1) * wk],
                              (((0,), (0,)), ((), ())), preferred_element_type=jnp.float32)
        s_ref[i] = decay[:, i * wk:(i + 1) * wk] * s_ref[i] + bd_ref[...] * upd


def _gla_kernel(qf, kf, vf, mf, qb, kb, vb, mb, wgf, bgf, wgb, bgb, tri_f, tri_b, ge, bd,
                of, ob, cf_ref, cb_ref, sf_ref, sb_ref):
    @pl.when(pl.program_id(0) == 0)
    def _():
        sf_ref[...] = jnp.zeros(sf_ref.shape, jnp.float32)
        sb_ref[...] = jnp.zeros(sb_ref.shape, jnp.float32)

    def cum_log_decay(m_ref, w_ref, b_ref, tri_ref):
        z = _dot_split(m_ref[...], w_ref[...]) + b_ref[...]
        la = (jnp.minimum(z, 0.0) - jnp.log1p(jnp.exp(-jnp.abs(z)))) * (1.0 / GLA_GATE_NORM)
        la_hi, la_lo = _split_bf16(la)
        tri = tri_ref[...]
        return _dot(tri, la_hi) + _dot(tri, la_lo)

    cf_ref[...] = cum_log_decay(mf, wgf, bgf, tri_f)
    cb_ref[...] = cum_log_decay(mb, wgb, bgb, tri_b)
    n_blocks = qf.shape[0] // GLA_BLOCK

    def body(t, carry):
        r_f = pl.multiple_of(t * GLA_BLOCK, GLA_BLOCK)
        r_b = pl.multiple_of((n_blocks - 1 - t) * GLA_BLOCK, GLA_BLOCK)
        _gla_block(qf, kf, vf, cf_ref, of, sf_ref, ge, bd, r_f, True)
        _gla_block(qb, kb, vb, cb_ref, ob, sb_ref, ge, bd, r_b, False)
        return carry

    lax.fori_loop(0, n_blocks, body, 0)


def gla_scan(p, w_gf_p, b_gf, w_gb_p, b_gb, consts):
    N = p.shape[0]
    tm = ROW_TILE
    nt = N // tm
    nk, nv = GLA_HEADS * GLA_DK, GLA_HEADS * GLA_DV
    fwd = lambda i: (i + nt - 1) % nt
    bwd = lambda i: (2 * nt - 1 - i) % nt
    col = lambda w, c, order: pl.BlockSpec((tm, w), lambda i: (order(i), c))
    full = lambda a: pl.BlockSpec(a.shape, lambda i: (0,) * a.ndim)
    side = lambda order: [col(nk, COL_LQ // nk, order), col(nk, COL_LK // nk, order),
                          col(nv, COL_LV // nv, order), col(LANES, COL_MISC // LANES, order)]
    small = (w_gf_p, b_gf, w_gb_p, b_gb) + tuple(consts)
    return pl.pallas_call(
        _gla_kernel,
        grid=(nt,),
        in_specs=side(fwd) + side(bwd) + [full(a) for a in small],
        out_specs=[pl.BlockSpec((tm, nv), lambda i: (fwd(i), 0)),
                   pl.BlockSpec((tm, nv), lambda i: (bwd(i), 0))],
        out_shape=[jax.ShapeDtypeStruct((N, nv), jnp.float32)] * 2,
        scratch_shapes=[pltpu.VMEM((tm, nk), jnp.float32), pltpu.VMEM((tm, nk), jnp.float32),
                        pltpu.VMEM((GLA_HEADS // 2, 2 * GLA_DV, 2 * GLA_DK), jnp.float32),
                        pltpu.VMEM((GLA_HEADS // 2, 2 * GLA_DV, 2 * GLA_DK), jnp.float32)],
        compiler_params=_cparams(("arbitrary",)),
        name="gla_scan",
    )(p, p, p, p, p, p, p, p, *small)


def _gla_consts():
    i = jnp.arange(ROW_TILE)
    same = (i[:, None] // GLA_BLOCK) == (i[None, :] // GLA_BLOCK)
    tri_f = (same & (i[None, :] <= i[:, None])).astype(jnp.bfloat16)
    tri_b = (same & (i[None, :] >= i[:, None])).astype(jnp.bfloat16)
    hk = jnp.arange(GLA_HEADS * GLA_DK) // GLA_DK
    hv = jnp.arange(GLA_HEADS * GLA_DV) // GLA_DV
    ge = (hk[:, None] == hv[None, :]).astype(jnp.bfloat16)
    bd = (hv[:2 * GLA_DV, None] == hk[None, :2 * GLA_DK]).astype(jnp.float32)
    return tri_f, tri_b, ge, bd


def _prep_gate(w_f, b_f, w_b, b_b):
    z = jnp.zeros((LANES, w_f.shape[1]), jnp.float32)
    return (z.at[MLA_ROPE:MLA_ROPE + GLA_GATE_RANK].set(w_f), b_f[None],
            z.at[MLA_ROPE + GLA_GATE_RANK:MLA_ROPE + 2 * GLA_GATE_RANK].set(w_b), b_b[None])


def _layer_norm(u, g, b):
    mu = jnp.mean(u, axis=-1, keepdims=True)
    d = u - mu
    var = jnp.mean(d * d, axis=-1, keepdims=True)
    return d * lax.rsqrt(var + LN_EPS) * g + b


def _first_max(rows):
    best, idx = rows[0], jnp.zeros(rows[0].shape, jnp.int32)
    for i in range(1, len(rows)):
        better = rows[i] > best
        idx = jnp.where(better, i, idx)
        best = jnp.where(better, rows[i], best)
    return best, idx


def _route(lt):
    grp_rows = [lt[g:g + 1, :] for g in range(N_GROUPS)]
    g_best, grp = _first_max(grp_rows)
    grp_w = 1.0 / sum(jnp.exp(r - g_best) for r in grp_rows)
    in_grp = []
    for j in range(EXPERTS_PER_GROUP):
        val = lt[8 + j:9 + j, :]
        for g in range(1, N_GROUPS):
            r = 8 + g * EXPERTS_PER_GROUP + j
            val = jnp.where(grp == g, lt[r:r + 1, :], val)
        in_grp.append(val)
    v1, i1 = _first_max(in_grp)
    rest = [jnp.where(i1 == j, -jnp.inf, in_grp[j]) for j in range(EXPERTS_PER_GROUP)]
    v2, i2 = _first_max(rest)
    e = jnp.exp(v2 - v1)
    w1 = 1.0 / (1.0 + e)
    ids = jnp.concatenate([grp * EXPERTS_PER_GROUP + i1, grp * EXPERTS_PER_GROUP + i2], axis=0)
    gates = jnp.concatenate([grp_w * w1, grp_w * (e * w1)], axis=0)
    return ids, gates


def _outproj_kernel(n_lat_tiles, alpha,
                    x_ref, yml, ymc, ygl, ygc, of_ref, ob_ref, g_ref, on_ref, wo_ref,
                    g1_ref, sh2_ref, sc2_ref, lg_ref, lb_ref, wr_ref, br_ref,
                    x1_ref, h2_ref, id_ref, gt_ref):
    is_ctx = pl.program_id(0) >= n_lat_tiles
    y_mla = jnp.where(is_ctx, ymc[...], yml[...])
    y_gqa = jnp.where(is_ctx, ygc[...], ygl[...])
    o = of_ref[...] + ob_ref[...]
    heads = [_rms(o[:, h * GLA_DV:(h + 1) * GLA_DV], on_ref[...]) for h in range(GLA_HEADS)]
    g = g_ref[...]
    y_gla = (jnp.concatenate(heads, axis=1) * (g * jax.nn.sigmoid(g))).astype(jnp.bfloat16)
    w_mla, w_gqa = y_mla.shape[1], y_gqa.shape[1]
    y = (_dot(y_mla, wo_ref[0:w_mla, :]) + _dot(y_gqa, wo_ref[w_mla:w_mla + w_gqa, :])
         + _dot(y_gla, wo_ref[w_mla + w_gqa:, :]))
    x1 = _layer_norm(alpha * x_ref[...] + _mod_row(g1_ref, is_ctx) * y, lg_ref[...], lb_ref[...])
    x1_ref[...] = x1
    h2 = x1 * (1.0 + _mod_row(sc2_ref, is_ctx)) + _mod_row(sh2_ref, is_ctx)
    h2_ref[...] = h2
    lt = _dot_split(wr_ref[...], h2, _dot_nt) + br_ref[...]
    ids, gates = _route(lt)
    rid = lax.broadcasted_iota(jnp.int32, id_ref.shape, 0)
    id_ref[...] = jnp.where(rid == 0, ids[0:1], jnp.where(rid == 1, ids[1:2], 0))
    gt_ref[...] = jnp.where(rid == 0, gates[0:1], jnp.where(rid == 1, gates[1:2], 0.0))


def out_projection(xs, y_mla, y_gqa, o_f, o_b, p, out_norm, w_out, mod_l, ln_g, ln_b, w_rt, b_rt,
                   n_lat, alpha):
    N, D = xs.shape
    tm = ROW_TILE
    nl = n_lat // tm
    assert N - n_lat == tm
    nv = GLA_HEADS * GLA_DV
    lat = lambda a: pl.BlockSpec((tm, a.shape[1]), lambda i: (jnp.minimum(i, nl - 1), 0))
    ctx = lambda a: pl.BlockSpec((tm, a.shape[1]), lambda i: (0, 0))
    row = lambda w, c: pl.BlockSpec((tm, w), lambda i: (i, c))
    full = lambda a: pl.BlockSpec(a.shape, lambda i: (0,) * a.ndim)
    modc = lambda c: pl.BlockSpec((8, D), lambda i: (0, c))
    (yml, ymc), (ygl, ygc) = y_mla, y_gqa
    return pl.pallas_call(
        functools.partial(_outproj_kernel, nl, alpha),
        grid=(N // tm,),
        in_specs=[row(D, 0), lat(yml), ctx(ymc), lat(ygl), ctx(ygc), row(nv, 0), row(nv, 0),
                  row(nv, COL_LG // nv), full(out_norm), full(w_out),
                  modc(2), modc(3), modc(4), full(ln_g), full(ln_b), full(w_rt), full(b_rt)],
        out_specs=[row(D, 0), row(D, 0), pl.BlockSpec((8, tm), lambda i: (0, i)),
                   pl.BlockSpec((8, tm), lambda i: (0, i))],
        out_shape=[jax.ShapeDtypeStruct((N, D), jnp.float32), jax.ShapeDtypeStruct((N, D), jnp.float32),
                   jax.ShapeDtypeStruct((8, N), jnp.int32), jax.ShapeDtypeStruct((8, N), jnp.float32)],
        compiler_params=_cparams(("arbitrary",)),
        name="out_projection",
    )(xs, yml, ymc, ygl, ygc, o_f, o_b, p, out_norm, w_out, mod_l, mod_l, mod_l, ln_g, ln_b, w_rt, b_rt)


def _row_copy(src_hbm, idx_ref, dst, sem, r):
    return pltpu.make_async_copy(src_hbm.at[pl.ds(idx_ref[0, 0, r], 1), :], dst.at[pl.ds(r, 1), :], sem)


def _start_rows(src_hbm, idx_ref, dst, sem, unrolled):
    n = dst.shape[0]
    if unrolled:
        for r in range(n):
            _row_copy(src_hbm, idx_ref, dst, sem, r).start(priority=r % 2)
    else:
        def start(r, c):
            _row_copy(src_hbm, idx_ref, dst, sem, r).start()
            return c
        lax.fori_loop(0, n, start, 0, unroll=8)


def _wait_rows(src_hbm, dst, sem):
    pltpu.make_async_copy(src_hbm.at[pl.ds(0, dst.shape[0]), :], dst, sem).wait()


def _double_buffered_gather(i, n_steps, src_hbm, idx_ref, idx_next_ref, bufs, sems, compute):
    @pl.when(i == 0)
    def _():
        _start_rows(src_hbm, idx_ref, bufs[0], sems.at[0], unrolled=False)

    for parity in (0, 1):
        cur, nxt = bufs[parity], bufs[1 - parity]
        for has_next in (True, False):
            more = i + 1 < n_steps
            cond = (i < n_steps) & (i % 2 == parity) & (more if has_next else jnp.logical_not(more))

            @pl.when(cond)
            def _(cur=cur, nxt=nxt, parity=parity, has_next=has_next):
                _wait_rows(src_hbm, cur, sems.at[parity])
                if has_next:
                    _start_rows(src_hbm, idx_next_ref, nxt, sems.at[1 - parity], unrolled=True)
                compute(cur)


def _expert_kernel(te_ref, nu_ref, idx_ref, idx_next_ref, h_hbm, wg_ref, wu_ref, wd_ref, y_ref,
                   xa, xb, wg_bf, wu_bf, wd_bf, sems):
    i = pl.program_id(0)
    n_used = nu_ref[0]

    @pl.when((i < n_used) & ((i == 0) | (te_ref[i] != te_ref[jnp.maximum(i - 1, 0)])))
    def _():
        wg_bf[...] = wg_ref[0, 0].astype(jnp.bfloat16)
        wu_bf[...] = wu_ref[0, 0].astype(jnp.bfloat16)
        wd_bf[...] = wd_ref[0, 0].astype(jnp.bfloat16)

    def compute(x_ref):
        x = x_ref[...].astype(jnp.bfloat16)
        a = _dot(x, wg_bf[...])
        u = _dot(x, wu_bf[...])
        hid = (a * jax.nn.sigmoid(a) * u).astype(jnp.bfloat16)
        y_ref[...] = _dot(hid, wd_bf[...])

    _double_buffered_gather(i, n_used, h_hbm, idx_ref, idx_next_ref, (xa, xb), sems, compute)

    @pl.when(i >= n_used)
    def _():
        y_ref[...] = jnp.zeros(y_ref.shape, jnp.float32)


def expert_ffn(h2, tile_expert, n_used, sorted_tok, w_g, w_u, w_d, layer):
    N, D = h2.shape
    tm = MOE_TILE
    n_tiles = sorted_tok.shape[0] // tm
    F = w_g.shape[3]
    bf = jnp.bfloat16
    idx = sorted_tok.reshape(n_tiles, 1, tm)
    nxt = lambda i, te, nu: (jnp.minimum(i + 1, n_tiles - 1), 0, 0)
    gs = pltpu.PrefetchScalarGridSpec(
        num_scalar_prefetch=2,
        grid=(n_tiles,),
        in_specs=[pl.BlockSpec((1, 1, tm), lambda i, te, nu: (i, 0, 0), memory_space=pltpu.SMEM),
                  pl.BlockSpec((1, 1, tm), nxt, memory_space=pltpu.SMEM),
                  pl.BlockSpec(memory_space=pl.ANY),
                  pl.BlockSpec((1, 1, D, F), lambda i, te, nu: (layer, te[i], 0, 0)),
                  pl.BlockSpec((1, 1, D, F), lambda i, te, nu: (layer, te[i], 0, 0)),
                  pl.BlockSpec((1, 1, F, D), lambda i, te, nu: (layer, te[i], 0, 0))],
        out_specs=pl.BlockSpec((tm, D), lambda i, te, nu: (i, 0)),
        scratch_shapes=[pltpu.VMEM((tm, D), jnp.float32), pltpu.VMEM((tm, D), jnp.float32),
                        pltpu.VMEM((D, F), bf), pltpu.VMEM((D, F), bf), pltpu.VMEM((F, D), bf),
                        pltpu.SemaphoreType.DMA((2,))],
    )
    return pl.pallas_call(
        _expert_kernel,
        grid_spec=gs,
        out_shape=jax.ShapeDtypeStruct((n_tiles * tm, D), jnp.float32),
        compiler_params=_cparams(("arbitrary",)),
        name="expert_ffn",
    )(tile_expert, n_used, idx, idx, h2, w_g, w_u, w_d)


def _combine_kernel(n_lat_tiles, alpha, pos_ref, pos_next_ref, y_hbm, x_ref, gc_ref, g2_ref, lg_ref,
                    lb_ref, o_ref, ya, yb, sems):
    i = pl.program_id(0)
    is_ctx = i >= n_lat_tiles
    tm = x_ref.shape[0]

    def compute(y_ref):
        gc = gc_ref[...]
        f = gc[:, 0:1] * y_ref[0:tm, :] + gc[:, 1:2] * y_ref[tm:2 * tm, :]
        o_ref[...] = _layer_norm(alpha * x_ref[...] + _mod_row(g2_ref, is_ctx) * f, lg_ref[...], lb_ref[...])

    _double_buffered_gather(i, pl.num_programs(0), y_hbm, pos_ref, pos_next_ref, (ya, yb), sems, compute)


def moe_combine(xs1, y_sorted, pos, gate_cols, mod_l, ln_g, ln_b, n_lat, alpha):
    N, D = xs1.shape
    tm = ROW_TILE
    nt = N // tm
    full = lambda a: pl.BlockSpec(a.shape, lambda i: (0,) * a.ndim)
    return pl.pallas_call(
        functools.partial(_combine_kernel, n_lat // tm, alpha),
        grid=(nt,),
        in_specs=[pl.BlockSpec((1, 1, 2 * tm), lambda i: (i, 0, 0), memory_space=pltpu.SMEM),
                  pl.BlockSpec((1, 1, 2 * tm), lambda i: (jnp.minimum(i + 1, nt - 1), 0, 0),
                               memory_space=pltpu.SMEM),
                  pl.BlockSpec(memory_space=pl.ANY),
                  pl.BlockSpec((tm, D), lambda i: (i, 0)),
                  pl.BlockSpec((tm, 2), lambda i: (i, 0)),
                  pl.BlockSpec((8, D), lambda i: (0, 5)), full(ln_g), full(ln_b)],
        out_specs=pl.BlockSpec((tm, D), lambda i: (i, 0)),
        out_shape=jax.ShapeDtypeStruct((N, D), jnp.float32),
        scratch_shapes=[pltpu.VMEM((2 * tm, D), jnp.float32), pltpu.VMEM((2 * tm, D), jnp.float32),
                        pltpu.SemaphoreType.DMA((2,))],
        compiler_params=_cparams(("arbitrary",)),
        name="moe_combine",
    )(pos, pos, y_sorted, xs1, gate_cols, mod_l, ln_g, ln_b)


def _dispatch_plan(ids, n_tok):
    tm = MOE_TILE
    n_asg = 2 * n_tok
    assert n_asg * n_asg < 2 ** 31
    n_tiles = n_asg // tm + N_EXPERTS
    i32 = jnp.int32
    e_flat = ids.reshape(n_asg)
    ar = jnp.arange(n_asg, dtype=i32)
    order = jnp.sort(e_flat * n_asg + ar) % n_asg
    rank = jnp.sort(order * n_asg + ar) % n_asg
    experts = jnp.arange(N_EXPERTS, dtype=i32)
    onehot = e_flat[:, None] == experts[None, :]
    counts = jnp.sum(onehot, axis=0, dtype=i32)
    ends = jnp.cumsum(counts).astype(i32)
    padded = ((counts + tm - 1) // tm) * tm
    pad_ends = jnp.cumsum(padded).astype(i32)
    shift = (pad_ends - padded) - (ends - counts)
    pos = rank + jnp.sum(jnp.where(onehot, shift[None, :], 0), axis=1, dtype=i32)
    n_used = (pad_ends[-1] // tm).reshape(1)
    tile_start = jnp.arange(n_tiles, dtype=i32) * tm
    tile_expert = jnp.sum(tile_start[:, None] >= pad_ends[None, :], axis=1, dtype=i32)
    last = jnp.sum((pad_ends[-1] - 1) >= pad_ends, dtype=i32)
    tile_expert = jnp.minimum(tile_expert, last)
    of_tile = tile_expert[:, None] == experts[None, :]
    shift_t = jnp.sum(jnp.where(of_tile, shift[None, :], 0), axis=1, dtype=i32)
    ends_t = jnp.sum(jnp.where(of_tile, ends[None, :], 0), axis=1, dtype=i32)
    r = tile_start[:, None] + jnp.arange(tm, dtype=i32)[None, :] - shift_t[:, None]
    valid = (r < ends_t[:, None]) & (tile_start[:, None] < pad_ends[-1])
    sorted_tok = jnp.where(valid, order[jnp.clip(r, 0, n_asg - 1)] % n_tok, 0).reshape(n_tiles * tm)
    nt = n_tok // ROW_TILE
    pos_tiles = jnp.concatenate([pos[:n_tok].reshape(nt, 1, ROW_TILE),
                                 pos[n_tok:].reshape(nt, 1, ROW_TILE)], axis=2)
    return sorted_tok, tile_expert, n_used, pos_tiles


def _rope_tables(n_lat, n_ctx, dim):
    rows = n_lat // GRID_W
    row = jnp.repeat(jnp.arange(rows), GRID_W).astype(jnp.float32)
    col = jnp.tile(jnp.arange(GRID_W), rows).astype(jnp.float32)
    half = dim // 2
    inv_freq = ROPE_THETA ** (-jnp.arange(0, half, 2, dtype=jnp.float32) / half)
    ang_r = row[:, None] * inv_freq
    ang_c = col[:, None] * inv_freq
    ang = jnp.concatenate([ang_r, ang_r, ang_c, ang_c], axis=-1)
    cos, sin = jnp.cos(ang), jnp.sin(ang)
    q = dim // 4
    lane = jnp.arange(dim)
    first = ((lane // q) % 2) == 0
    sin_a = jnp.where(first, -sin, 0.0)
    sin_b = jnp.where(first, 0.0, sin)

    def pad(t, fill):
        t = jnp.pad(t, ((0, 0), (0, LANES - dim)), constant_values=fill)
        return jnp.pad(t, ((0, n_ctx), (0, 0)), constant_values=fill)

    return pad(cos, 1.0), pad(sin_a, 0.0), pad(sin_b, 0.0)


def _in_perm():
    a = MLA_Q_RANK + MLA_KV_RANK + MLA_ROPE
    b = a + (GQA_HEADS + 2 * GQA_KV_HEADS) * HEAD_DIM
    nk, nv = GLA_HEADS * GLA_DK, GLA_HEADS * GLA_DV
    r = lambda s, n: list(range(s, s + n))
    cols = (r(0, MLA_Q_RANK + MLA_KV_RANK)
            + r(a, b - a)
            + r(b, 2 * nk + 2 * nv)
            + r(MLA_Q_RANK + MLA_KV_RANK, MLA_ROPE)
            + r(b + 2 * nk + 2 * nv, 2 * GLA_GATE_RANK))
    return jnp.asarray(cols, jnp.int32)


def _prep_w_in(w_in_l):
    w = w_in_l[:, _in_perm()]
    scale = jnp.ones((w.shape[1],), jnp.float32).at[COL_LQ:COL_LK].set(GLA_DK ** -0.5)
    w = w * scale
    return jnp.pad(w, ((0, 0), (0, P_COLS - w.shape[1]))).astype(jnp.bfloat16)


def _prep_w_uq(w_uq_l):
    r = w_uq_l.shape[0]
    w = w_uq_l.reshape(r, MLA_HEADS, MLA_NOPE + MLA_ROPE)
    w = jnp.pad(w, ((0, 0), (0, 0), (0, MLA_QPAD - MLA_NOPE - MLA_ROPE)))
    return w.reshape(r, MLA_HEADS * MLA_QPAD).astype(jnp.bfloat16)


def _prep_router(w_rg, b_rg, w_re, b_re):
    D = w_rg.shape[0]
    w = jnp.zeros((LANES, D), jnp.float32).at[0:N_GROUPS].set(w_rg.T).at[8:8 + N_EXPERTS].set(w_re.T)
    b = jnp.zeros((LANES,), jnp.float32).at[0:N_GROUPS].set(b_rg).at[8:8 + N_EXPERTS].set(b_re)
    return w, b[:, None]


def _q_split(n_lat, want_tiles):
    tiles = n_lat // ROW_TILE
    q_tiles = want_tiles if tiles % want_tiles == 0 else 1
    q_groups = max(g for g in (4, 2, 1) if (tiles // q_tiles) % g == 0)
    return dict(q_tiles=q_tiles, q_groups=q_groups)


def _kv_tile(n):
    return max(t for t in (256, 640, 1280) if n % t == 0)


def kernel(x, c, ctx, c_ctx, ada_w, ada_b, w_in, mla_q_norm, mla_w_uq, mla_kv_norm, mla_w_ukv, gqa_q_norm, gqa_k_norm, gla_w_gate_fwd, gla_b_gate_fwd, gla_w_gate_bwd, gla_b_gate_bwd, gla_out_norm, w_out, ln1_g, ln1_b, w_route_group, b_route_group, w_route_expert, b_route_expert, w_expert_gate, w_expert_up, w_expert_down, ln2_g, ln2_b):
    T, C = x.shape[1], ctx.shape[1]
    N = T + C
    L = ada_w.shape[0]
    assert x.shape[0] == 1 and C == ROW_TILE and T % ROW_TILE == 0 and T % GRID_W == 0
    alpha = (2.0 * L) ** 0.25
    bf = jnp.bfloat16

    xs = jnp.concatenate([x[0], ctx[0]], axis=0)
    mods = modulation(c, c_ctx, ada_w, ada_b)
    tabs_mla = _rope_tables(T, C, MLA_ROPE)
    tabs_gqa = _rope_tables(T, C, HEAD_DIM)
    gla_consts = _gla_consts()
    mla_scale = (MLA_NOPE + MLA_ROPE) ** -0.5 * LOG2E
    gqa_scale = HEAD_DIM ** -0.5 * LOG2E
    tabs_mla_t = tuple(t.T for t in tabs_mla)
    tabs_gqa_t = tuple(t.T for t in tabs_gqa)
    lat = dict(q_row0=0, n_q=T, k_row0=0, n_k=N, tks=_kv_tile(N))
    con = dict(q_row0=T, n_q=C, k_row0=T, n_k=C, tks=C)

    for l in range(L):
        mod_l = mods[l]
        p = in_projection(xs, mod_l, _prep_w_in(w_in[l]), T)

        w_ukv = mla_w_ukv[l].reshape(MLA_KV_RANK, MLA_HEADS, MLA_NOPE + MLA_V)
        w_kn = w_ukv[:, :, :MLA_NOPE].reshape(MLA_KV_RANK, MLA_HEADS * MLA_NOPE).astype(bf)
        w_v_t = w_ukv[:, :, MLA_NOPE:].reshape(MLA_KV_RANK, MLA_HEADS * MLA_V).T.astype(bf)
        qt, k, vt = mla_prep(p, (mla_q_norm[l] * mla_scale)[None], _prep_w_uq(mla_w_uq[l]).T,
                             mla_kv_norm[l][None], w_kn, w_v_t, tabs_mla, tabs_mla_t)
        y_mla = (flash_attention(qt, k, vt, group=1, **_q_split(T, 4), **lat),
                 flash_attention(qt, k, vt, group=1, q_tiles=1, q_groups=1, **con))
        qt, k, vt = gqa_prep(p, (gqa_q_norm[l] * gqa_scale)[:, None], gqa_k_norm[l][None],
                             tabs_gqa, tabs_gqa_t)
        y_gqa = (flash_attention(qt, k, vt, group=GQA_GROUP, **_q_split(T, 1), **lat),
                 flash_attention(qt, k, vt, group=GQA_GROUP, q_tiles=1, q_groups=1, **con))
        o_f, o_b = gla_scan(p, *_prep_gate(gla_w_gate_fwd[l], gla_b_gate_fwd[l],
                                           gla_w_gate_bwd[l], gla_b_gate_bwd[l]), gla_consts)

        w_rt, b_rt = _prep_router(w_route_group[l], b_route_group[l], w_route_expert[l], b_route_expert[l])
        x1, h2, ids, gates = out_projection(xs, y_mla, y_gqa, o_f, o_b, p, gla_out_norm[l][None],
                                            w_out[l].astype(bf), mod_l, ln1_g[l][None], ln1_b[l][None],
                                            w_rt, b_rt, T, alpha)

        sorted_tok, tile_expert, n_used, pos_tiles = _dispatch_plan(ids[0:2], N)
        y_sorted = expert_ffn(h2, tile_expert, n_used, sorted_tok, w_expert_gate, w_expert_up,
                              w_expert_down, l)
        xs = moe_combine(x1, y_sorted, pos_tiles, gates[0:2].T, mod_l, ln2_g[l][None], ln2_b[l][None],
                         T, alpha)
    return xs[:T][None]
```
